```python
import math
import jax, jax.numpy as jnp
from jax import lax
import numpy as np

D_MODEL = 1024
BATCH = 16
SEQ = 2048
DEPTH = 1

HGRN_HEADS = 8
HGRN_DK = 64
HGRN_DV = 64
HGRN_KEY = HGRN_HEADS * HGRN_DK
HGRN_WIDTH = HGRN_HEADS * HGRN_DV
SB_HEADS = 8
SB_DH = 64
SB_WIDTH = SB_HEADS * SB_DH
D_MIX = HGRN_WIDTH + SB_WIDTH
IN_COLS = 2 * HGRN_KEY + 2 * HGRN_WIDTH + 3 * SB_WIDTH
CHUNK = 64
Q_BLOCK = 128
D_FF = -(-8 * D_MODEL // (3 * 256)) * 256
EPS = 1e-6

kernel_name = "hymba_hgrn2_stickbreaking_block"


def rmsnorm(x, g):
    xf = x.astype(jnp.float32)
    y = xf * lax.rsqrt(jnp.mean(xf * xf, axis=-1, keepdims=True) + EPS)
    return (y * g.astype(jnp.float32)).astype(x.dtype)


def hgrn2_mix(q, f_logit, inp, gate, lb, norm_g):
    B, S, _ = q.shape
    n_chunks = S // CHUNK
    dt = q.dtype

    def heads(t, d):
        t = t.astype(jnp.float32).reshape(B, n_chunks, CHUNK, HGRN_HEADS, d)
        return jnp.transpose(t, (0, 3, 1, 2, 4))

    lbf = lb.astype(jnp.float32)
    f = lbf + (1.0 - lbf) * jax.nn.sigmoid(f_logit.astype(jnp.float32))
    qh = heads(q, HGRN_DK)
    fh = heads(f, HGRN_DK)
    kh = 1.0 - fh
    vh = heads(inp, HGRN_DV)
    b = jnp.cumsum(jnp.log(fh), axis=3)
    b_end = b[:, :, :, -1:, :]

    qe = qh * jnp.exp(b)
    ke = kh * jnp.exp(-b)
    scores = jnp.einsum('bhnid,bhnjd->bhnij', qe, ke)
    tri = jnp.tril(jnp.ones((CHUNK, CHUNK), dtype=bool))
    scores = jnp.where(tri, scores, 0.0)
    o_intra = jnp.einsum('bhnij,bhnje->bhnie', scores, vh)

    kv = jnp.einsum('bhncd,bhnce->bhnde', kh * jnp.exp(b_end - b), vh)
    decay = jnp.exp(b_end[:, :, :, 0, :])

    def step(state, xs):
        dec, upd = xs
        return dec[..., None] * state + upd, state

    s0 = jnp.zeros((B, HGRN_HEADS, HGRN_DK, HGRN_DV), jnp.float32)
    _, s_prev = lax.scan(step, s0, (jnp.moveaxis(decay, 2, 0), jnp.moveaxis(kv, 2, 0)))
    s_prev = jnp.moveaxis(s_prev, 0, 2)
    o_inter = jnp.einsum('bhncd,bhnde->bhnce', qe, s_prev)

    o = o_intra + o_inter
    o = jnp.transpose(o, (0, 2, 3, 1, 4)).reshape(B, S, HGRN_HEADS, HGRN_DV)
    o = o * lax.rsqrt(jnp.mean(o * o, axis=-1, keepdims=True) + EPS)
    o = o.reshape(B, S, HGRN_WIDTH) * norm_g.astype(jnp.float32)
    o = o * jax.nn.silu(gate.astype(jnp.float32))
    return o.astype(dt)


def stick_breaking_mix(q, k, v, norm_g):
    B, S, _ = q.shape
    dt = q.dtype

    def heads(t):
        return jnp.transpose(t.astype(jnp.float32).reshape(B, S, SB_HEADS, SB_DH), (0, 2, 1, 3))

    qh, kh, vh = heads(q), heads(k), heads(v)
    scale = 1.0 / math.sqrt(SB_DH)
    outs = []
    for blk in range(S // Q_BLOCK):
        start = blk * Q_BLOCK
        end = start + Q_BLOCK
        qb = qh[:, :, start:end]
        kb = kh[:, :, :end]
        vb = vh[:, :, :end]
        z = jnp.einsum('bhqd,bhkd->bhqk', qb, kb) * scale
        q_pos = start + jnp.arange(Q_BLOCK)
        k_pos = jnp.arange(end)
        mask = k_pos[None, :] < q_pos[:, None]
        c = jnp.where(mask, jax.nn.softplus(z), 0.0)
        after = lax.cumsum(c, axis=3, reverse=True) - c
        attn = jnp.where(mask, jnp.exp(jax.nn.log_sigmoid(z) - after), 0.0)
        outs.append(jnp.einsum('bhqk,bhkd->bhqd', attn, vb))
    o = jnp.concatenate(outs, axis=2)
    o = jnp.transpose(o, (0, 2, 1, 3))
    o = o * lax.rsqrt(jnp.mean(o * o, axis=-1, keepdims=True) + EPS)
    o = o.reshape(B, S, SB_WIDTH) * norm_g.astype(jnp.float32)
    return o.astype(dt)


def setup_inputs(seed: int = 0) -> dict:
    key = jax.random.key(seed)
    ks = jax.random.split(key, 12)
    f32 = jnp.float32

    def w(k, shape, fan_in):
        return jax.random.normal(k, shape, f32) * fan_in ** -0.5

    def gain(k, shape):
        return 1.0 + 0.01 * jax.random.normal(k, shape, f32)

    return {
        "x": jax.random.normal(ks[0], (BATCH, SEQ, D_MODEL), f32),
        "mix_norm_g": gain(ks[1], (DEPTH, D_MODEL)),
        "w_in": w(ks[2], (DEPTH, D_MODEL, IN_COLS), D_MODEL),
        "lower_bounds": 0.1 * jax.random.normal(ks[3], (DEPTH + 1, HGRN_KEY), f32),
        "hgrn_norm_g": gain(ks[4], (DEPTH, HGRN_WIDTH)),
        "sb_norm_g": gain(ks[5], (DEPTH, SB_WIDTH)),
        "w_out": w(ks[6], (DEPTH, D_MIX, D_MODEL), D_MIX),
        "ffn_norm_g": gain(ks[7], (DEPTH, D_MODEL)),
        "w_gate": w(ks[8], (DEPTH, D_MODEL, D_FF), D_MODEL),
        "w_up": w(ks[9], (DEPTH, D_MODEL, D_FF), D_MODEL),
        "w_down": w(ks[10], (DEPTH, D_FF, D_MODEL), D_FF),
        "final_norm_g": gain(ks[11], (D_MODEL,)),
    }


def reference(x, mix_norm_g, w_in, lower_bounds, hgrn_norm_g, sb_norm_g, w_out,
              ffn_norm_g, w_gate, w_up, w_down, final_norm_g):
    lb_all = jnp.cumsum(jax.nn.softmax(lower_bounds.astype(jnp.float32), axis=0), axis=0)
    splits = np.cumsum([HGRN_KEY, HGRN_KEY, HGRN_WIDTH, HGRN_WIDTH,
                        SB_WIDTH, SB_WIDTH])
    for l in range(DEPTH):
        h = rmsnorm(x, mix_norm_g[l])
        proj = jnp.einsum('bsd,de->bse', h, w_in[l])
        hq, hf, hi, hg, sq, sk, sv = jnp.split(proj, splits, axis=-1)
        o_a = hgrn2_mix(hq, hf, hi, hg, lb_all[l].astype(x.dtype), hgrn_norm_g[l])
        o_b = stick_breaking_mix(sq, sk, sv, sb_norm_g[l])
        mixed = jnp.concatenate([o_a, o_b], axis=-1)
        x = x + jnp.einsum('bse,ed->bsd', mixed, w_out[l])
        h = rmsnorm(x, ffn_norm_g[l])
        ff = jax.nn.silu(jnp.einsum('bsd,df->bsf', h, w_gate[l])) * jnp.einsum('bsd,df->bsf', h, w_up[l])
        x = x + jnp.einsum('bsf,fd->bsd', ff, w_down[l])
    return rmsnorm(x, final_norm_g)
```

```python
import functools

import jax
import jax.numpy as jnp
from jax import lax
from jax.experimental import pallas as pl
from jax.experimental.pallas import tpu as pltpu

F32 = jnp.float32
BF16 = jnp.bfloat16

LANES = 128
FF_CHUNK = 256
HEAD_DIM = 64
HEADS_PER_GROUP = 8
GROUP = HEADS_PER_GROUP * HEAD_DIM
PAIRS = GROUP // LANES
HGRN_GROUPS = 4
ATTN_GROUPS = 3
CHUNK = 64
KEY_BLOCK = 128
Q_BLOCK = 128
EPS = 1e-6
VMEM_LIMIT = 56 * 1024 * 1024

_NT = (((1,), (1,)), ((), ()))
_TN = (((0,), (0,)), ((), ()))


def _resident(shape):
    return pl.BlockSpec(shape, lambda *_: (0,) * len(shape), pipeline_mode=pl.Buffered(1))


def _rms(x, gain):
    return x * lax.rsqrt(jnp.mean(x * x, axis=-1, keepdims=True) + EPS) * gain


def _pair_rms(o, first_head):
    sq = o * o
    s0 = jnp.sum(jnp.where(first_head, sq, 0.0), axis=-1, keepdims=True)
    s1 = jnp.sum(jnp.where(first_head, 0.0, sq), axis=-1, keepdims=True)
    ms = jnp.where(first_head, s0, s1) * (1.0 / HEAD_DIM)
    return o * lax.rsqrt(ms + EPS)


def _norm_proj_kernel(x_ref, g_ref, w_ref, ph_ref, pa_ref):
    h = _rms(x_ref[...], g_ref[...]).astype(BF16)
    for c in range(HGRN_GROUPS):
        cols = slice(c * GROUP, (c + 1) * GROUP)
        ph_ref[:, cols] = jnp.dot(h, w_ref[:, cols], preferred_element_type=F32)
    base = HGRN_GROUPS * GROUP
    for c in range(ATTN_GROUPS):
        r = jnp.dot(h, w_ref[:, base + c * GROUP: base + (c + 1) * GROUP],
                    preferred_element_type=F32)
        if c == 0:
            r = r * (1.0 / HEAD_DIM ** 0.5)
        pa_ref[:, c * GROUP:(c + 1) * GROUP] = r.astype(BF16)


def _norm_proj(x2d, gain, w_in, tm=512):
    t, d = x2d.shape
    n_h, n_a = HGRN_GROUPS * GROUP, ATTN_GROUPS * GROUP
    return pl.pallas_call(
        _norm_proj_kernel,
        grid=(t // tm,),
        in_specs=[pl.BlockSpec((tm, d), lambda i: (i, 0)),
                  _resident((1, d)),
                  _resident((d, n_h + n_a))],
        out_specs=[pl.BlockSpec((tm, n_h), lambda i: (i, 0)),
                   pl.BlockSpec((tm, n_a), lambda i: (i, 0))],
        out_shape=[jax.ShapeDtypeStruct((t, n_h), F32),
                   jax.ShapeDtypeStruct((t, n_a), BF16)],
        compiler_params=pltpu.CompilerParams(
            dimension_semantics=("arbitrary",), vmem_limit_bytes=VMEM_LIMIT),
        name="norm_proj",
    )(x2d, gain, w_in)


def _hgrn_kernel(layer, q_ref, f_ref, i_ref, g_ref, lbp_ref, ng_ref, o_ref, st_ref):
    seq = q_ref.shape[1]
    lane = lax.broadcasted_iota(jnp.int32, (1, LANES), 1)
    first_head = lane < HEAD_DIM

    lbp = lbp_ref[...]
    ex = jnp.exp(lbp - jnp.max(lbp, axis=0, keepdims=True))
    sm = ex / jnp.sum(ex, axis=0, keepdims=True)
    lb = jnp.sum(sm[:layer + 1], axis=0, keepdims=True)

    r64 = lax.broadcasted_iota(jnp.int32, (CHUNK, CHUNK), 0)
    c64 = lax.broadcasted_iota(jnp.int32, (CHUNK, CHUNK), 1)
    cum = jnp.where(c64 <= r64, 1.0, 0.0).astype(BF16)
    row = lax.broadcasted_iota(jnp.int32, (CHUNK, LANES), 0)
    col = lax.broadcasted_iota(jnp.int32, (CHUNK, LANES), 1)
    causal = (col & (HEAD_DIM - 1)) <= row
    r128 = lax.broadcasted_iota(jnp.int32, (LANES, LANES), 0)
    c128 = lax.broadcasted_iota(jnp.int32, (LANES, LANES), 1)
    same_head = (r128 < HEAD_DIM) == (c128 < HEAD_DIM)

    st_ref[...] = jnp.zeros_like(st_ref)

    def split_heads_rows(a):
        zero = jnp.zeros_like(a)
        return jnp.concatenate([jnp.where(first_head, a, zero),
                                jnp.where(first_head, zero, a)], axis=0)

    def chunk(n, carry):
        rows = pl.ds(pl.multiple_of(n * CHUNK, CHUNK), CHUNK)
        f = lb + (1.0 - lb) * jax.nn.sigmoid(f_ref[0, rows, :])
        logf = jnp.log(f)
        hi = logf.astype(BF16)
        rem = logf - hi.astype(F32)
        mid = rem.astype(BF16)
        lo = (rem - mid.astype(F32)).astype(BF16)
        b = (jnp.dot(cum, hi, preferred_element_type=F32)
             + jnp.dot(cum, mid, preferred_element_type=F32)
             + jnp.dot(cum, lo, preferred_element_type=F32))
        b_end = b[CHUNK - 1:CHUNK, :]
        k = 1.0 - f
        qe = (q_ref[0, rows, :] * jnp.exp(b)).astype(BF16)
        ke = (k * jnp.exp(-b)).astype(BF16)
        ke_end = (k * jnp.exp(b_end - b)).astype(BF16)
        v = i_ref[0, rows, :].astype(BF16)

        scores = lax.dot_general(qe, split_heads_rows(ke), _NT, preferred_element_type=F32)
        scores = jnp.where(causal, scores, 0.0).astype(BF16)
        o = jnp.dot(scores, split_heads_rows(v), preferred_element_type=F32)
        st = st_ref[...]
        o = o + lax.dot_general(qe, st.astype(BF16), _NT, preferred_element_type=F32)

        kv_t = lax.dot_general(v, ke_end, _TN, preferred_element_type=F32)
        st_ref[...] = st * jnp.exp(b_end) + jnp.where(same_head, kv_t, 0.0)

        gate = g_ref[0, rows, :]
        o = _pair_rms(o, first_head) * ng_ref[...] * (gate * jax.nn.sigmoid(gate))
        o_ref[0, rows, :] = o.astype(o_ref.dtype)
        return carry

    lax.fori_loop(0, seq // CHUNK, chunk, 0)


def _hgrn(proj_h, lower_bounds, norm_g, layer):
    b, s, _ = proj_h.shape
    n_lb = lower_bounds.shape[0]

    def group(c):
        return pl.BlockSpec((1, s, LANES), lambda bi, p, c=c: (bi, 0, c * PAIRS + p))

    return pl.pallas_call(
        functools.partial(_hgrn_kernel, layer),
        grid=(b, PAIRS),
        in_specs=[group(0), group(1), group(2), group(3),
                  pl.BlockSpec((n_lb, LANES), lambda bi, p: (0, p)),
                  pl.BlockSpec((1, LANES), lambda bi, p: (0, p))],
        out_specs=pl.BlockSpec((1, s, LANES), lambda bi, p: (bi, 0, p)),
        out_shape=jax.ShapeDtypeStruct((b, s, GROUP), BF16),
        scratch_shapes=[pltpu.VMEM((LANES, LANES), F32)],
        compiler_params=pltpu.CompilerParams(
            dimension_semantics=("arbitrary", "arbitrary"), vmem_limit_bytes=VMEM_LIMIT),
        name="hgrn2",
    )(proj_h, proj_h, proj_h, proj_h, lower_bounds, norm_g)


def _attn_kernel(q_ref, k_ref, v_ref, tri_ref, ng_ref, o_ref, kk_ref, vv_ref):
    qi = pl.program_id(2)
    n_blocks = k_ref.shape[1] // KEY_BLOCK
    lane = lax.broadcasted_iota(jnp.int32, (1, LANES), 1)
    first_head = lane < HEAD_DIM

    @pl.when(qi == 0)
    def _split_heads():
        def body(j, carry):
            rows = pl.ds(pl.multiple_of(j * KEY_BLOCK, KEY_BLOCK), KEY_BLOCK)
            kb = k_ref[0, rows, :]
            vb = v_ref[0, rows, :]
            zero = jnp.zeros_like(kb)
            kk_ref[j, 0:KEY_BLOCK, :] = jnp.where(first_head, kb, zero)
            kk_ref[j, KEY_BLOCK:2 * KEY_BLOCK, :] = jnp.where(first_head, zero, kb)
            vv_ref[j, 0:KEY_BLOCK, :] = jnp.where(first_head, vb, zero)
            vv_ref[j, KEY_BLOCK:2 * KEY_BLOCK, :] = jnp.where(first_head, zero, vb)
            return carry
        lax.fori_loop(0, n_blocks, body, 0)

    q = q_ref[0]
    row = lax.broadcasted_iota(jnp.int32, (Q_BLOCK, 2 * KEY_BLOCK), 0)
    col = lax.broadcasted_iota(jnp.int32, (Q_BLOCK, 2 * KEY_BLOCK), 1)
    strictly_before = (col & (KEY_BLOCK - 1)) < row

    def block(j, run, acc, diagonal):
        z = lax.dot_general(q, kk_ref[j], _NT, preferred_element_type=F32)
        sp = jnp.maximum(z, 0.0) + jnp.log(1.0 + jnp.exp(-jnp.abs(z)))
        c = jnp.where(strictly_before, sp, 0.0) if diagonal else sp
        cb = c.astype(BF16)
        later = jnp.dot(cb, tri_ref[...], preferred_element_type=F32)
        p = jnp.exp(z - sp - (later + run))
        if diagonal:
            p = jnp.where(strictly_before, p, 0.0)
        acc = acc + jnp.dot(p.astype(BF16), vv_ref[j], preferred_element_type=F32)
        tot0 = later[:, 0:1] + cb[:, 0:1].astype(F32)
        tot1 = later[:, KEY_BLOCK:KEY_BLOCK + 1] + cb[:, KEY_BLOCK:KEY_BLOCK + 1].astype(F32)
        run = run + jnp.concatenate([jnp.broadcast_to(tot0, (Q_BLOCK, KEY_BLOCK)),
                                     jnp.broadcast_to(tot1, (Q_BLOCK, KEY_BLOCK))], axis=1)
        return run, acc

    run0 = jnp.zeros((Q_BLOCK, 2 * KEY_BLOCK), F32)
    acc0 = jnp.zeros((Q_BLOCK, LANES), F32)
    run, acc = block(qi, run0, acc0, True)

    def body(t, carry):
        return block(qi - 1 - t, carry[0], carry[1], False)

    run, acc = lax.fori_loop(0, qi, body, (run, acc))
    o_ref[0] = (_pair_rms(acc, first_head) * ng_ref[...]).astype(o_ref.dtype)


def _attention(proj_a, tri, norm_g):
    b, s, _ = proj_a.shape
    n_blocks = s // KEY_BLOCK
    return pl.pallas_call(
        _attn_kernel,
        grid=(b, PAIRS, s // Q_BLOCK),
        in_specs=[pl.BlockSpec((1, Q_BLOCK, LANES), lambda bi, p, i: (bi, i, p)),
                  pl.BlockSpec((1, s, LANES), lambda bi, p, i: (bi, 0, PAIRS + p)),
                  pl.BlockSpec((1, s, LANES), lambda bi, p, i: (bi, 0, 2 * PAIRS + p)),
                  _resident((2 * KEY_BLOCK, 2 * KEY_BLOCK)),
                  pl.BlockSpec((1, LANES), lambda bi, p, i: (0, p))],
        out_specs=pl.BlockSpec((1, Q_BLOCK, LANES), lambda bi, p, i: (bi, i, p)),
        out_shape=jax.ShapeDtypeStruct((b, s, GROUP), BF16),
        scratch_shapes=[pltpu.VMEM((n_blocks, 2 * KEY_BLOCK, LANES), BF16),
                        pltpu.VMEM((n_blocks, 2 * KEY_BLOCK, LANES), BF16)],
        compiler_params=pltpu.CompilerParams(
            dimension_semantics=("arbitrary", "arbitrary", "arbitrary"),
            vmem_limit_bytes=VMEM_LIMIT),
        name="stickbreak_attn",
    )(proj_a, proj_a, proj_a, tri, norm_g)


def _suffix_sum_matrix():
    r = jnp.arange(2 * KEY_BLOCK)[:, None]
    c = jnp.arange(2 * KEY_BLOCK)[None, :]
    return ((r > c) & ((r < KEY_BLOCK) == (c < KEY_BLOCK))).astype(BF16)


def _out_ffn_kernel(final, x_ref, ma_ref, mb_ref, wo_ref, g2_ref, wg_ref, wu_ref, wd_ref,
                    g3_ref, o_ref, ff_ref):
    x1 = (x_ref[...]
          + jnp.dot(ma_ref[...], wo_ref[0:GROUP, :], preferred_element_type=F32)
          + jnp.dot(mb_ref[...], wo_ref[GROUP:2 * GROUP, :], preferred_element_type=F32))
    h = _rms(x1, g2_ref[...]).astype(BF16)
    d_ff = wg_ref.shape[1]
    for c in range(d_ff // FF_CHUNK):
        cols = slice(c * FF_CHUNK, (c + 1) * FF_CHUNK)
        gate = jnp.dot(h, wg_ref[:, cols], preferred_element_type=F32)
        up = jnp.dot(h, wu_ref[:, cols], preferred_element_type=F32)
        ff_ref[:, cols] = (gate * jax.nn.sigmoid(gate) * up).astype(BF16)
    x2 = x1 + jnp.dot(ff_ref[...], wd_ref[...], preferred_element_type=F32)
    o_ref[...] = _rms(x2, g3_ref[...]) if final else x2


def _out_ffn(x2d, mix_a, mix_b, w_out, g2, w_gate, w_up, w_down, g3, final, tm=512):
    t, d = x2d.shape
    d_ff = w_gate.shape[1]
    row = lambda n: pl.BlockSpec((tm, n), lambda i: (i, 0))
    return pl.pallas_call(
        functools.partial(_out_ffn_kernel, final),
        grid=(t // tm,),
        in_specs=[row(d), row(GROUP), row(GROUP),
                  _resident((2 * GROUP, d)), _resident((1, d)),
                  _resident((d, d_ff)), _resident((d, d_ff)), _resident((d_ff, d)),
                  _resident((1, d))],
        out_specs=row(d),
        out_shape=jax.ShapeDtypeStruct((t, d), F32),
        scratch_shapes=[pltpu.VMEM((tm, d_ff), BF16)],
        compiler_params=pltpu.CompilerParams(
            dimension_semantics=("arbitrary",), vmem_limit_bytes=VMEM_LIMIT),
        name="out_ffn",
    )(x2d, mix_a, mix_b, w_out, g2, w_gate, w_up, w_down, g3)


def kernel(x, mix_norm_g, w_in, lower_bounds, hgrn_norm_g, sb_norm_g, w_out, ffn_norm_g,
           w_gate, w_up, w_down, final_norm_g):
    b, s, d = x.shape
    depth = w_in.shape[0]
    assert w_in.shape[2] == (HGRN_GROUPS + ATTN_GROUPS) * GROUP
    assert w_gate.shape[2] % FF_CHUNK == 0 and s % KEY_BLOCK == 0
    tri = _suffix_sum_matrix()
    x2d = x.reshape(b * s, d)
    for l in range(depth):
        proj_h, proj_a = _norm_proj(x2d, mix_norm_g[l][None], w_in[l].astype(BF16))
        mix_a = _hgrn(proj_h.reshape(b, s, -1), lower_bounds, hgrn_norm_g[l][None], l)
        mix_b = _attention(proj_a.reshape(b, s, -1), tri, sb_norm_g[l][None])
        last = l == depth - 1
        g3 = final_norm_g[None] if last else ffn_norm_g[l][None]
        x2d = _out_ffn(x2d, mix_a.reshape(b * s, -1), mix_b.reshape(b * s, -1),
                       w_out[l].astype(BF16), ffn_norm_g[l][None], w_gate[l].astype(BF16),
                       w_up[l].astype(BF16), w_down[l].astype(BF16), g3, last)
    if depth == 0:
        raise ValueError("depth must be positive")
    return x2d.reshape(b, s, d)
```

```python
import functools

import jax
import jax.numpy as jnp
from jax import lax
from jax.experimental import pallas as pl
from jax.experimental.pallas import tpu as pltpu

F32 = jnp.float32
BF16 = jnp.bfloat16

LANES = 128
FF_CHUNK = 256
HEAD_DIM = 64
HEADS_PER_GROUP = 8
GROUP = HEADS_PER_GROUP * HEAD_DIM
PAIRS = GROUP // LANES
HGRN_GROUPS = 4
ATTN_GROUPS = 3
CHUNK = 64
KEY_BLOCK = 128
Q_BLOCK = 512
EPS = 1e-6
VMEM_LIMIT = 56 * 1024 * 1024

_NT = (((1,), (1,)), ((), ()))
_TN = (((0,), (0,)), ((), ()))


def _resident(shape):
    return pl.BlockSpec(shape, lambda *_: (0,) * len(shape), pipeline_mode=pl.Buffered(1))


def _rms(x, gain):
    return x * lax.rsqrt(jnp.mean(x * x, axis=-1, keepdims=True) + EPS) * gain


def _pair_rms(o, first_head):
    sq = o * o
    s0 = jnp.sum(jnp.where(first_head, sq, 0.0), axis=-1, keepdims=True)
    s1 = jnp.sum(jnp.where(first_head, 0.0, sq), axis=-1, keepdims=True)
    ms = jnp.where(first_head, s0, s1) * (1.0 / HEAD_DIM)
    return o * lax.rsqrt(ms + EPS)


def _norm_proj_kernel(x_ref, g_ref, w_ref, ph_ref, pa_ref):
    h = _rms(x_ref[...], g_ref[...]).astype(BF16)
    for c in range(HGRN_GROUPS):
        cols = slice(c * GROUP, (c + 1) * GROUP)
        ph_ref[:, cols] = jnp.dot(h, w_ref[:, cols], preferred_element_type=F32)
    base = HGRN_GROUPS * GROUP
    for c in range(ATTN_GROUPS):
        r = jnp.dot(h, w_ref[:, base + c * GROUP: base + (c + 1) * GROUP],
                    preferred_element_type=F32)
        if c == 0:
            r = r * (1.0 / HEAD_DIM ** 0.5)
        pa_ref[:, c * GROUP:(c + 1) * GROUP] = r.astype(BF16)


def _norm_proj(x2d, gain, w_in, tm=512):
    t, d = x2d.shape
    n_h, n_a = HGRN_GROUPS * GROUP, ATTN_GROUPS * GROUP
    return pl.pallas_call(
        _norm_proj_kernel,
        grid=(t // tm,),
        in_specs=[pl.BlockSpec((tm, d), lambda i: (i, 0)),
                  _resident((1, d)),
                  _resident((d, n_h + n_a))],
        out_specs=[pl.BlockSpec((tm, n_h), lambda i: (i, 0)),
                   pl.BlockSpec((tm, n_a), lambda i: (i, 0))],
        out_shape=[jax.ShapeDtypeStruct((t, n_h), F32),
                   jax.ShapeDtypeStruct((t, n_a), BF16)],
        compiler_params=pltpu.CompilerParams(
            dimension_semantics=("arbitrary",), vmem_limit_bytes=VMEM_LIMIT),
        name="norm_proj",
    )(x2d, gain, w_in)


def _hgrn_kernel(layer, q_ref, f_ref, i_ref, g_ref, lbp_ref, ng_ref, o_ref, st_ref):
    seq = q_ref.shape[1]
    lane = lax.broadcasted_iota(jnp.int32, (1, LANES), 1)
    first_head = lane < HEAD_DIM

    lbp = lbp_ref[...]
    ex = jnp.exp(lbp - jnp.max(lbp, axis=0, keepdims=True))
    sm = ex / jnp.sum(ex, axis=0, keepdims=True)
    lb = jnp.sum(sm[:layer + 1], axis=0, keepdims=True)

    r64 = lax.broadcasted_iota(jnp.int32, (CHUNK, CHUNK), 0)
    c64 = lax.broadcasted_iota(jnp.int32, (CHUNK, CHUNK), 1)
    cum = jnp.where(c64 <= r64, 1.0, 0.0).astype(BF16)
    row = lax.broadcasted_iota(jnp.int32, (CHUNK, LANES), 0)
    col = lax.broadcasted_iota(jnp.int32, (CHUNK, LANES), 1)
    causal = (col & (HEAD_DIM - 1)) <= row
    r128 = lax.broadcasted_iota(jnp.int32, (LANES, LANES), 0)
    c128 = lax.broadcasted_iota(jnp.int32, (LANES, LANES), 1)
    same_head = (r128 < HEAD_DIM) == (c128 < HEAD_DIM)

    st_ref[...] = jnp.zeros_like(st_ref)

    def split_heads_rows(a):
        zero = jnp.zeros_like(a)
        return jnp.concatenate([jnp.where(first_head, a, zero),
                                jnp.where(first_head, zero, a)], axis=0)

    def chunk(n, carry):
        rows = pl.ds(pl.multiple_of(n * CHUNK, CHUNK), CHUNK)
        f = lb + (1.0 - lb) * jax.nn.sigmoid(f_ref[0, rows, :])
        logf = jnp.log(f)
        hi = logf.astype(BF16)
        rem = logf - hi.astype(F32)
        mid = rem.astype(BF16)
        lo = (rem - mid.astype(F32)).astype(BF16)
        b = (jnp.dot(cum, hi, preferred_element_type=F32)
             + jnp.dot(cum, mid, preferred_element_type=F32)
             + jnp.dot(cum, lo, preferred_element_type=F32))
        b_end = b[CHUNK - 1:CHUNK, :]
        k = 1.0 - f
        qe = (q_ref[0, rows, :] * jnp.exp(b)).astype(BF16)
        ke = (k * jnp.exp(-b)).astype(BF16)
        ke_end = (k * jnp.exp(b_end - b)).astype(BF16)
        v = i_ref[0, rows, :].astype(BF16)

        scores = lax.dot_general(qe, split_heads_rows(ke), _NT, preferred_element_type=F32)
        scores = jnp.where(causal, scores, 0.0).astype(BF16)
        o = jnp.dot(scores, split_heads_rows(v), preferred_element_type=F32)
        st = st_ref[...]
        o = o + lax.dot_general(qe, st.astype(BF16), _NT, preferred_element_type=F32)

        kv_t = lax.dot_general(v, ke_end, _TN, preferred_element_type=F32)
        st_ref[...] = st * jnp.exp(b_end) + jnp.where(same_head, kv_t, 0.0)

        gate = g_ref[0, rows, :]
        o = _pair_rms(o, first_head) * ng_ref[...] * (gate * jax.nn.sigmoid(gate))
        o_ref[0, rows, :] = o.astype(o_ref.dtype)
        return carry

    lax.fori_loop(0, seq // CHUNK, chunk, 0)


def _hgrn(proj_h, lower_bounds, norm_g, layer):
    b, s, _ = proj_h.shape
    n_lb = lower_bounds.shape[0]

    def group(c):
        return pl.BlockSpec((1, s, LANES), lambda bi, p, c=c: (bi, 0, c * PAIRS + p))

    return pl.pallas_call(
        functools.partial(_hgrn_kernel, layer),
        grid=(b, PAIRS),
        in_specs=[group(0), group(1), group(2), group(3),
                  pl.BlockSpec((n_lb, LANES), lambda bi, p: (0, p)),
                  pl.BlockSpec((1, LANES), lambda bi, p: (0, p))],
        out_specs=pl.BlockSpec((1, s, LANES), lambda bi, p: (bi, 0, p)),
        out_shape=jax.ShapeDtypeStruct((b, s, GROUP), BF16),
        scratch_shapes=[pltpu.VMEM((LANES, LANES), F32)],
        compiler_params=pltpu.CompilerParams(
            dimension_semantics=("arbitrary", "arbitrary"), vmem_limit_bytes=VMEM_LIMIT),
        name="hgrn2",
    )(proj_h, proj_h, proj_h, proj_h, lower_bounds, norm_g)


def _attn_kernel(q_ref, k_ref, v_ref, tri_ref, ng_ref, o_ref, kk_ref, vv_ref):
    qi = pl.program_id(1)
    n_blocks = k_ref.shape[1] // KEY_BLOCK
    diag_blocks = Q_BLOCK // KEY_BLOCK
    lane = lax.broadcasted_iota(jnp.int32, (1, LANES), 1)
    first_head = lane < HEAD_DIM

    @pl.when(qi == 0)
    def _split_heads():
        def body(j, carry):
            rows = pl.ds(pl.multiple_of(j * KEY_BLOCK, KEY_BLOCK), KEY_BLOCK)
            for p in range(PAIRS):
                kb = k_ref[0, rows, p * LANES:(p + 1) * LANES]
                vb = v_ref[0, rows, p * LANES:(p + 1) * LANES]
                zero = jnp.zeros_like(kb)
                kk_ref[p, j, 0:KEY_BLOCK, :] = jnp.where(first_head, kb, zero)
                kk_ref[p, j, KEY_BLOCK:2 * KEY_BLOCK, :] = jnp.where(first_head, zero, kb)
                vv_ref[p, j, 0:KEY_BLOCK, :] = jnp.where(first_head, vb, zero)
                vv_ref[p, j, KEY_BLOCK:2 * KEY_BLOCK, :] = jnp.where(first_head, zero, vb)
            return carry
        lax.fori_loop(0, n_blocks, body, 0)

    row = lax.broadcasted_iota(jnp.int32, (Q_BLOCK, 2 * KEY_BLOCK), 0)
    col = lax.broadcasted_iota(jnp.int32, (Q_BLOCK, 2 * KEY_BLOCK), 1)
    key_in_block = col & (KEY_BLOCK - 1)

    def block(j, carry, diag):
        if diag is not None:
            strictly_before = key_in_block + diag * KEY_BLOCK < row
        out = []
        for p in range(PAIRS):
            run, acc = carry[p]
            q = q_ref[0, :, p * LANES:(p + 1) * LANES]
            z = lax.dot_general(q, kk_ref[p, j], _NT, preferred_element_type=F32)
            sp = jnp.maximum(z, 0.0) + jnp.log(1.0 + jnp.exp(-jnp.abs(z)))
            c = sp if diag is None else jnp.where(strictly_before, sp, 0.0)
            cb = c.astype(BF16)
            later = jnp.dot(cb, tri_ref[...], preferred_element_type=F32)
            pr = jnp.exp(z - sp - (later + run))
            if diag is not None:
                pr = jnp.where(strictly_before, pr, 0.0)
            acc = acc + jnp.dot(pr.astype(BF16), vv_ref[p, j], preferred_element_type=F32)
            tot0 = later[:, 0:1] + cb[:, 0:1].astype(F32)
            tot1 = later[:, KEY_BLOCK:KEY_BLOCK + 1] + cb[:, KEY_BLOCK:KEY_BLOCK + 1].astype(F32)
            run = run + jnp.concatenate([jnp.broadcast_to(tot0, (Q_BLOCK, KEY_BLOCK)),
                                         jnp.broadcast_to(tot1, (Q_BLOCK, KEY_BLOCK))], axis=1)
            out.append((run, acc))
        return tuple(out)

    carry = tuple((jnp.zeros((Q_BLOCK, 2 * KEY_BLOCK), F32), jnp.zeros((Q_BLOCK, LANES), F32))
                  for _ in range(PAIRS))
    first_key_block = qi * diag_blocks
    for diag in reversed(range(diag_blocks)):
        carry = block(first_key_block + diag, carry, diag)
    carry = lax.fori_loop(0, first_key_block,
                          lambda t, c: block(first_key_block - 1 - t, c, None), carry)
    for p in range(PAIRS):
        cols = slice(p * LANES, (p + 1) * LANES)
        o_ref[0, :, cols] = (_pair_rms(carry[p][1], first_head) * ng_ref[:, cols]).astype(o_ref.dtype)


def _attention(proj_a, tri, norm_g):
    b, s, _ = proj_a.shape
    n_blocks = s // KEY_BLOCK
    return pl.pallas_call(
        _attn_kernel,
        grid=(b, s // Q_BLOCK),
        in_specs=[pl.BlockSpec((1, Q_BLOCK, GROUP), lambda bi, i: (bi, i, 0)),
                  pl.BlockSpec((1, s, GROUP), lambda bi, i: (bi, 0, 1)),
                  pl.BlockSpec((1, s, GROUP), lambda bi, i: (bi, 0, 2)),
                  _resident((2 * KEY_BLOCK, 2 * KEY_BLOCK)),
                  _resident((1, GROUP))],
        out_specs=pl.BlockSpec((1, Q_BLOCK, GROUP), lambda bi, i: (bi, i, 0)),
        out_shape=jax.ShapeDtypeStruct((b, s, GROUP), BF16),
        scratch_shapes=[pltpu.VMEM((PAIRS, n_blocks, 2 * KEY_BLOCK, LANES), BF16),
                        pltpu.VMEM((PAIRS, n_blocks, 2 * KEY_BLOCK, LANES), BF16)],
        compiler_params=pltpu.CompilerParams(
            dimension_semantics=("arbitrary", "arbitrary"), vmem_limit_bytes=VMEM_LIMIT),
        name="stickbreak_attn",
    )(proj_a, proj_a, proj_a, tri, norm_g)


def _suffix_sum_matrix():
    r = jnp.arange(2 * KEY_BLOCK)[:, None]
    c = jnp.arange(2 * KEY_BLOCK)[None, :]
    return ((r > c) & ((r < KEY_BLOCK) == (c < KEY_BLOCK))).astype(BF16)


def _out_ffn_kernel(final, x_ref, ma_ref, mb_ref, wo_ref, g2_ref, wg_ref, wu_ref, wd_ref,
                    g3_ref, o_ref, ff_ref):
    x1 = (x_ref[...]
          + jnp.dot(ma_ref[...], wo_ref[0:GROUP, :], preferred_element_type=F32)
          + jnp.dot(mb_ref[...], wo_ref[GROUP:2 * GROUP, :], preferred_element_type=F32))
    h = _rms(x1, g2_ref[...]).astype(BF16)
    d_ff = wg_ref.shape[1]
    for c in range(d_ff // FF_CHUNK):
        cols = slice(c * FF_CHUNK, (c + 1) * FF_CHUNK)
        gate = jnp.dot(h, wg_ref[:, cols], preferred_element_type=F32)
        up = jnp.dot(h, wu_ref[:, cols], preferred_element_type=F32)
        ff_ref[:, cols] = (gate * jax.nn.sigmoid(gate) * up).astype(BF16)
    x2 = x1 + jnp.dot(ff_ref[...], wd_ref[...], preferred_element_type=F32)
    o_ref[...] = _rms(x2, g3_ref[...]) if final else x2


def _out_ffn(x2d, mix_a, mix_b, w_out, g2, w_gate, w_up, w_down, g3, final, tm=512):
    t, d = x2d.shape
    d_ff = w_gate.shape[1]
    row = lambda n: pl.BlockSpec((tm, n), lambda i: (i, 0))
    return pl.pallas_call(
        functools.partial(_out_ffn_kernel, final),
        grid=(t // tm,),
        in_specs=[row(d), row(GROUP), row(GROUP),
                  _resident((2 * GROUP, d)), _resident((1, d)),
                  _resident((d, d_ff)), _resident((d, d_ff)), _resident((d_ff, d)),
                  _resident((1, d))],
        out_specs=row(d),
        out_shape=jax.ShapeDtypeStruct((t, d), F32),
        scratch_shapes=[pltpu.VMEM((tm, d_ff), BF16)],
        compiler_params=pltpu.CompilerParams(
            dimension_semantics=("arbitrary",), vmem_limit_bytes=VMEM_LIMIT),
        name="out_ffn",
    )(x2d, mix_a, mix_b, w_out, g2, w_gate, w_up, w_down, g3)


def kernel(x, mix_norm_g, w_in, lower_bounds, hgrn_norm_g, sb_norm_g, w_out, ffn_norm_g,
           w_gate, w_up, w_down, final_norm_g):
    b, s, d = x.shape
    depth = w_in.shape[0]
    assert w_in.shape[2] == (HGRN_GROUPS + ATTN_GROUPS) * GROUP
    assert w_gate.shape[2] % FF_CHUNK == 0 and s % KEY_BLOCK == 0
    tri = _suffix_sum_matrix()
    x2d = x.reshape(b * s, d)
    for l in range(depth):
        proj_h, proj_a = _norm_proj(x2d, mix_norm_g[l][None], w_in[l].astype(BF16))
        mix_a = _hgrn(proj_h.reshape(b, s, -1), lower_bounds, hgrn_norm_g[l][None], l)
        mix_b = _attention(proj_a.reshape(b, s, -1), tri, sb_norm_g[l][None])
        last = l == depth - 1
        g3 = final_norm_g[None] if last else ffn_norm_g[l][None]
        x2d = _out_ffn(x2d, mix_a.reshape(b * s, -1), mix_b.reshape(b * s, -1),
                       w_out[l].astype(BF16), ffn_norm_g[l][None], w_gate[l].astype(BF16),
                       w_up[l].astype(BF16), w_down[l].astype(BF16), g3, last)
    if depth == 0:
        raise ValueError("depth must be positive")
    return x2d.reshape(b, s, d)
```

```python
import functools

import jax
import jax.numpy as jnp
from jax import lax
from jax.experimental import pallas as pl
from jax.experimental.pallas import tpu as pltpu

F32 = jnp.float32
BF16 = jnp.bfloat16

LANES = 128
FF_CHUNK = 256
HEAD_DIM = 64
HEADS_PER_GROUP = 8
GROUP = HEADS_PER_GROUP * HEAD_DIM
PAIRS = GROUP // LANES
HGRN_GROUPS = 4
ATTN_GROUPS = 3
CHUNK = 64
HGRN_SEQ_BLOCK = 512
KEY_BLOCK = 128
Q_BLOCK = 512
EPS = 1e-6
VMEM_LIMIT = 56 * 1024 * 1024

_NT = (((1,), (1,)), ((), ()))
_TN = (((0,), (0,)), ((), ()))


def _resident(shape):
    return pl.BlockSpec(shape, lambda *_: (0,) * len(shape), pipeline_mode=pl.Buffered(1))


def _rms(x, gain):
    return x * lax.rsqrt(jnp.mean(x * x, axis=-1, keepdims=True) + EPS) * gain


def _pair_rms(o, first_head):
    sq = o * o
    s0 = jnp.sum(jnp.where(first_head, sq, 0.0), axis=-1, keepdims=True)
    s1 = jnp.sum(jnp.where(first_head, 0.0, sq), axis=-1, keepdims=True)
    ms = jnp.where(first_head, s0, s1) * (1.0 / HEAD_DIM)
    return o * lax.rsqrt(ms + EPS)


def _norm_proj_kernel(x_ref, g_ref, w_ref, ph_ref, pa_ref):
    h = _rms(x_ref[...], g_ref[...]).astype(BF16)
    for c in range(HGRN_GROUPS):
        cols = slice(c * GROUP, (c + 1) * GROUP)
        ph_ref[:, cols] = jnp.dot(h, w_ref[:, cols], preferred_element_type=F32)
    base = HGRN_GROUPS * GROUP
    for c in range(ATTN_GROUPS):
        r = jnp.dot(h, w_ref[:, base + c * GROUP: base + (c + 1) * GROUP],
                    preferred_element_type=F32)
        if c == 0:
            r = r * (1.0 / HEAD_DIM ** 0.5)
        pa_ref[:, c * GROUP:(c + 1) * GROUP] = r.astype(BF16)


def _norm_proj(x2d, gain, w_in, tm=512):
    t, d = x2d.shape
    n_h, n_a = HGRN_GROUPS * GROUP, ATTN_GROUPS * GROUP
    return pl.pallas_call(
        _norm_proj_kernel,
        grid=(t // tm,),
        in_specs=[pl.BlockSpec((tm, d), lambda i: (i, 0)),
                  _resident((1, d)),
                  _resident((d, n_h + n_a))],
        out_specs=[pl.BlockSpec((tm, n_h), lambda i: (i, 0)),
                   pl.BlockSpec((tm, n_a), lambda i: (i, 0))],
        out_shape=[jax.ShapeDtypeStruct((t, n_h), F32),
                   jax.ShapeDtypeStruct((t, n_a), BF16)],
        compiler_params=pltpu.CompilerParams(
            dimension_semantics=("arbitrary",), vmem_limit_bytes=VMEM_LIMIT),
        name="norm_proj",
    )(x2d, gain, w_in)


def _hgrn_kernel(layer, q_ref, f_ref, i_ref, g_ref, lbp_ref, ng_ref, o_ref, st_ref):
    seq = q_ref.shape[1]
    lane = lax.broadcasted_iota(jnp.int32, (1, LANES), 1)
    first_head = lane < HEAD_DIM

    lbp = lbp_ref[...]
    ex = jnp.exp(lbp - jnp.max(lbp, axis=0, keepdims=True))
    sm = ex / jnp.sum(ex, axis=0, keepdims=True)
    lb = jnp.sum(sm[:layer + 1], axis=0, keepdims=True)

    r64 = lax.broadcasted_iota(jnp.int32, (CHUNK, CHUNK), 0)
    c64 = lax.broadcasted_iota(jnp.int32, (CHUNK, CHUNK), 1)
    cum = jnp.where(c64 <= r64, 1.0, 0.0).astype(BF16)
    row = lax.broadcasted_iota(jnp.int32, (CHUNK, LANES), 0)
    col = lax.broadcasted_iota(jnp.int32, (CHUNK, LANES), 1)
    causal = (col & (HEAD_DIM - 1)) <= row
    r128 = lax.broadcasted_iota(jnp.int32, (LANES, LANES), 0)
    c128 = lax.broadcasted_iota(jnp.int32, (LANES, LANES), 1)
    same_head = (r128 < HEAD_DIM) == (c128 < HEAD_DIM)

    @pl.when(pl.program_id(1) == 0)
    def _reset_state():
        st_ref[...] = jnp.zeros_like(st_ref)

    def split_heads_rows(a):
        zero = jnp.zeros_like(a)
        return jnp.concatenate([jnp.where(first_head, a, zero),
                                jnp.where(first_head, zero, a)], axis=0)

    def chunk(n, carry):
        rows = pl.ds(pl.multiple_of(n * CHUNK, CHUNK), CHUNK)
        f = lb + (1.0 - lb) * jax.nn.sigmoid(f_ref[0, rows, :])
        logf = jnp.log(f)
        hi = logf.astype(BF16)
        lo = (logf - hi.astype(F32)).astype(BF16)
        b = (jnp.dot(cum, hi, preferred_element_type=F32)
             + jnp.dot(cum, lo, preferred_element_type=F32))
        b_end = b[CHUNK - 1:CHUNK, :]
        k = 1.0 - f
        qe_all = (q_ref[0, rows, :] * jnp.exp(b)).astype(BF16)
        ke_all = (k * jnp.exp(-b)).astype(BF16)
        ke_end_all = (k * jnp.exp(b_end - b)).astype(BF16)
        v_all = i_ref[0, rows, :].astype(BF16)
        decay = jnp.exp(b_end)
        gate = g_ref[0, rows, :]
        out_gain = ng_ref[...] * (gate * jax.nn.sigmoid(gate))

        for p in range(PAIRS):
            cols = slice(p * LANES, (p + 1) * LANES)
            qe, ke, ke_end, v = qe_all[:, cols], ke_all[:, cols], ke_end_all[:, cols], v_all[:, cols]
            scores = lax.dot_general(qe, split_heads_rows(ke), _NT, preferred_element_type=F32)
            scores = jnp.where(causal, scores, 0.0).astype(BF16)
            o = jnp.dot(scores, split_heads_rows(v), preferred_element_type=F32)
            st = st_ref[p]
            o = o + lax.dot_general(qe, st.astype(BF16), _NT, preferred_element_type=F32)
            kv_t = lax.dot_general(v, ke_end, _TN, preferred_element_type=F32)
            st_ref[p] = st * decay[:, cols] + jnp.where(same_head, kv_t, 0.0)
            o_ref[0, rows, cols] = (_pair_rms(o, first_head) * out_gain[:, cols]).astype(o_ref.dtype)
        return carry

    lax.fori_loop(0, seq // CHUNK, chunk, 0, unroll=4)


def _hgrn(proj_h, lower_bounds, norm_g, layer):
    b, s, _ = proj_h.shape
    n_lb = lower_bounds.shape[0]

    def group(c):
        return pl.BlockSpec((1, HGRN_SEQ_BLOCK, GROUP), lambda bi, si, c=c: (bi, si, c))

    return pl.pallas_call(
        functools.partial(_hgrn_kernel, layer),
        grid=(b, s // HGRN_SEQ_BLOCK),
        in_specs=[group(0), group(1), group(2), group(3),
                  _resident((n_lb, GROUP)), _resident((1, GROUP))],
        out_specs=pl.BlockSpec((1, HGRN_SEQ_BLOCK, GROUP), lambda bi, si: (bi, si, 0)),
        out_shape=jax.ShapeDtypeStruct((b, s, GROUP), BF16),
        scratch_shapes=[pltpu.VMEM((PAIRS, LANES, LANES), F32)],
        compiler_params=pltpu.CompilerParams(
            dimension_semantics=("arbitrary", "arbitrary"), vmem_limit_bytes=VMEM_LIMIT),
        name="hgrn2",
    )(proj_h, proj_h, proj_h, proj_h, lower_bounds, norm_g)


def _attn_kernel(q_ref, k_ref, v_ref, tri_ref, ng_ref, o_ref, kk_ref, vv_ref):
    qi = pl.program_id(1)
    n_blocks = k_ref.shape[1] // KEY_BLOCK
    diag_blocks = Q_BLOCK // KEY_BLOCK
    lane = lax.broadcasted_iota(jnp.int32, (1, LANES), 1)
    first_head = lane < HEAD_DIM

    @pl.when(qi == 0)
    def _split_heads():
        def body(j, carry):
            rows = pl.ds(pl.multiple_of(j * KEY_BLOCK, KEY_BLOCK), KEY_BLOCK)
            for p in range(PAIRS):
                kb = k_ref[0, rows, p * LANES:(p + 1) * LANES]
                vb = v_ref[0, rows, p * LANES:(p + 1) * LANES]
                zero = jnp.zeros_like(kb)
                kk_ref[p, j, 0:KEY_BLOCK, :] = jnp.where(first_head, kb, zero)
                kk_ref[p, j, KEY_BLOCK:2 * KEY_BLOCK, :] = jnp.where(first_head, zero, kb)
                vv_ref[p, j, 0:KEY_BLOCK, :] = jnp.where(first_head, vb, zero)
                vv_ref[p, j, KEY_BLOCK:2 * KEY_BLOCK, :] = jnp.where(first_head, zero, vb)
            return carry
        lax.fori_loop(0, n_blocks, body, 0)

    row = lax.broadcasted_iota(jnp.int32, (Q_BLOCK, 2 * KEY_BLOCK), 0)
    col = lax.broadcasted_iota(jnp.int32, (Q_BLOCK, 2 * KEY_BLOCK), 1)
    key_in_block = col & (KEY_BLOCK - 1)

    def block(j, carry, diag):
        if diag is not None:
            strictly_before = key_in_block + diag * KEY_BLOCK < row
        out = []
        for p in range(PAIRS):
            run, acc = carry[p]
            q = q_ref[0, :, p * LANES:(p + 1) * LANES]
            z = lax.dot_general(q, kk_ref[p, j], _NT, preferred_element_type=F32)
            sp = jnp.maximum(z, 0.0) + jnp.log(1.0 + jnp.exp(-jnp.abs(z)))
            c = sp if diag is None else jnp.where(strictly_before, sp, 0.0)
            cb = c.astype(BF16)
            later = jnp.dot(cb, tri_ref[...], preferred_element_type=F32)
            pr = jnp.exp(z - sp - (later + run))
            if diag is not None:
                pr = jnp.where(strictly_before, pr, 0.0)
            acc = acc + jnp.dot(pr.astype(BF16), vv_ref[p, j], preferred_element_type=F32)
            tot0 = later[:, 0:1] + cb[:, 0:1].astype(F32)
            tot1 = later[:, KEY_BLOCK:KEY_BLOCK + 1] + cb[:, KEY_BLOCK:KEY_BLOCK + 1].astype(F32)
            run = run + jnp.concatenate([jnp.broadcast_to(tot0, (Q_BLOCK, KEY_BLOCK)),
                                         jnp.broadcast_to(tot1, (Q_BLOCK, KEY_BLOCK))], axis=1)
            out.append((run, acc))
        return tuple(out)

    carry = tuple((jnp.zeros((Q_BLOCK, 2 * KEY_BLOCK), F32), jnp.zeros((Q_BLOCK, LANES), F32))
                  for _ in range(PAIRS))
    first_key_block = qi * diag_blocks
    for diag in reversed(range(diag_blocks)):
        carry = block(first_key_block + diag, carry, diag)
    carry = lax.fori_loop(0, first_key_block,
                          lambda t, c: block(first_key_block - 1 - t, c, None), carry)
    for p in range(PAIRS):
        cols = slice(p * LANES, (p + 1) * LANES)
        o_ref[0, :, cols] = (_pair_rms(carry[p][1], first_head) * ng_ref[:, cols]).astype(o_ref.dtype)


def _attention(proj_a, tri, norm_g):
    b, s, _ = proj_a.shape
    n_blocks = s // KEY_BLOCK
    return pl.pallas_call(
        _attn_kernel,
        grid=(b, s // Q_BLOCK),
        in_specs=[pl.BlockSpec((1, Q_BLOCK, GROUP), lambda bi, i: (bi, i, 0)),
                  pl.BlockSpec((1, s, GROUP), lambda bi, i: (bi, 0, 1)),
                  pl.BlockSpec((1, s, GROUP), lambda bi, i: (bi, 0, 2)),
                  _resident((2 * KEY_BLOCK, 2 * KEY_BLOCK)),
                  _resident((1, GROUP))],
        out_specs=pl.BlockSpec((1, Q_BLOCK, GROUP), lambda bi, i: (bi, i, 0)),
        out_shape=jax.ShapeDtypeStruct((b, s, GROUP), BF16),
        scratch_shapes=[pltpu.VMEM((PAIRS, n_blocks, 2 * KEY_BLOCK, LANES), BF16),
                        pltpu.VMEM((PAIRS, n_blocks, 2 * KEY_BLOCK, LANES), BF16)],
        compiler_params=pltpu.CompilerParams(
            dimension_semantics=("arbitrary", "arbitrary"), vmem_limit_bytes=VMEM_LIMIT),
        name="stickbreak_attn",
    )(proj_a, proj_a, proj_a, tri, norm_g)


def _suffix_sum_matrix():
    r = jnp.arange(2 * KEY_BLOCK)[:, None]
    c = jnp.arange(2 * KEY_BLOCK)[None, :]
    return ((r > c) & ((r < KEY_BLOCK) == (c < KEY_BLOCK))).astype(BF16)


def _out_ffn_kernel(final, x_ref, ma_ref, mb_ref, wo_ref, g2_ref, wg_ref, wu_ref, wd_ref,
                    g3_ref, o_ref, ff_ref):
    x1 = (x_ref[...]
          + jnp.dot(ma_ref[...], wo_ref[0:GROUP, :], preferred_element_type=F32)
          + jnp.dot(mb_ref[...], wo_ref[GROUP:2 * GROUP, :], preferred_element_type=F32))
    h = _rms(x1, g2_ref[...]).astype(BF16)
    d_ff = wg_ref.shape[1]
    for c in range(d_ff // FF_CHUNK):
        cols = slice(c * FF_CHUNK, (c + 1) * FF_CHUNK)
        gate = jnp.dot(h, wg_ref[:, cols], preferred_element_type=F32)
        up = jnp.dot(h, wu_ref[:, cols], preferred_element_type=F32)
        ff_ref[:, cols] = (gate * jax.nn.sigmoid(gate) * up).astype(BF16)
    x2 = x1 + jnp.dot(ff_ref[...], wd_ref[...], preferred_element_type=F32)
    o_ref[...] = _rms(x2, g3_ref[...]) if final else x2


def _out_ffn(x2d, mix_a, mix_b, w_out, g2, w_gate, w_up, w_down, g3, final, tm=512):
    t, d = x2d.shape
    d_ff = w_gate.shape[1]
    row = lambda n: pl.BlockSpec((tm, n), lambda i: (i, 0))
    return pl.pallas_call(
        functools.partial(_out_ffn_kernel, final),
        grid=(t // tm,),
        in_specs=[row(d), row(GROUP), row(GROUP),
                  _resident((2 * GROUP, d)), _resident((1, d)),
                  _resident((d, d_ff)), _resident((d, d_ff)), _resident((d_ff, d)),
                  _resident((1, d))],
        out_specs=row(d),
        out_shape=jax.ShapeDtypeStruct((t, d), F32),
        scratch_shapes=[pltpu.VMEM((tm, d_ff), BF16)],
        compiler_params=pltpu.CompilerParams(
            dimension_semantics=("arbitrary",), vmem_limit_bytes=VMEM_LIMIT),
        name="out_ffn",
    )(x2d, mix_a, mix_b, w_out, g2, w_gate, w_up, w_down, g3)


def kernel(x, mix_norm_g, w_in, lower_bounds, hgrn_norm_g, sb_norm_g, w_out, ffn_norm_g,
           w_gate, w_up, w_down, final_norm_g):
    b, s, d = x.shape
    depth = w_in.shape[0]
    assert w_in.shape[2] == (HGRN_GROUPS + ATTN_GROUPS) * GROUP
    assert w_gate.shape[2] % FF_CHUNK == 0 and s % KEY_BLOCK == 0
    tri = _suffix_sum_matrix()
    x2d = x.reshape(b * s, d)
    for l in range(depth):
        proj_h, proj_a = _norm_proj(x2d, mix_norm_g[l][None], w_in[l].astype(BF16))
        mix_a = _hgrn(proj_h.reshape(b, s, -1), lower_bounds, hgrn_norm_g[l][None], l)
        mix_b = _attention(proj_a.reshape(b, s, -1), tri, sb_norm_g[l][None])
        last = l == depth - 1
        g3 = final_norm_g[None] if last else ffn_norm_g[l][None]
        x2d = _out_ffn(x2d, mix_a.reshape(b * s, -1), mix_b.reshape(b * s, -1),
                       w_out[l].astype(BF16), ffn_norm_g[l][None], w_gate[l].astype(BF16),
                       w_up[l].astype(BF16), w_down[l].astype(BF16), g3, last)
    if depth == 0:
        raise ValueError("depth must be positive")
    return x2d.reshape(b, s, d)
```

```python
import functools

import jax
import jax.numpy as jnp
from jax import lax
from jax.experimental import pallas as pl
from jax.experimental.pallas import tpu as pltpu

F32 = jnp.float32
BF16 = jnp.bfloat16

LANES = 128
FF_CHUNK = 256
HEAD_DIM = 64
HEADS_PER_GROUP = 8
GROUP = HEADS_PER_GROUP * HEAD_DIM
PAIRS = GROUP // LANES
HGRN_GROUPS = 4
ATTN_GROUPS = 3
CHUNK = 64
HGRN_SEQ_BLOCK = 512
KEY_BLOCK = 128
Q_BLOCK = 512
KEY_BLOCKS_PER_ITER = 2
LOG2E = 1.4426950408889634
RUN_CUTOFF = 136.0
EPS = 1e-6
VMEM_LIMIT = 56 * 1024 * 1024

_NT = (((1,), (1,)), ((), ()))
_TN = (((0,), (0,)), ((), ()))


def _resident(shape):
    return pl.BlockSpec(shape, lambda *_: (0,) * len(shape), pipeline_mode=pl.Buffered(1))


def _rms(x, gain):
    return x * lax.rsqrt(jnp.mean(x * x, axis=-1, keepdims=True) + EPS) * gain


def _pair_rms(o, first_head):
    sq = o * o
    s0 = jnp.sum(jnp.where(first_head, sq, 0.0), axis=-1, keepdims=True)
    s1 = jnp.sum(jnp.where(first_head, 0.0, sq), axis=-1, keepdims=True)
    ms = jnp.where(first_head, s0, s1) * (1.0 / HEAD_DIM)
    return o * lax.rsqrt(ms + EPS)


def _norm_proj_kernel(x_ref, g_ref, w_ref, ph_ref, pa_ref):
    h = _rms(x_ref[...], g_ref[...]).astype(BF16)
    for c in range(HGRN_GROUPS):
        cols = slice(c * GROUP, (c + 1) * GROUP)
        ph_ref[:, cols] = jnp.dot(h, w_ref[:, cols], preferred_element_type=F32)
    base = HGRN_GROUPS * GROUP
    for c in range(ATTN_GROUPS):
        r = jnp.dot(h, w_ref[:, base + c * GROUP: base + (c + 1) * GROUP],
                    preferred_element_type=F32)
        if c == 0:
            r = r * (LOG2E / HEAD_DIM ** 0.5)
        pa_ref[:, c * GROUP:(c + 1) * GROUP] = r.astype(BF16)


def _norm_proj(x2d, gain, w_in, tm=512):
    t, d = x2d.shape
    n_h, n_a = HGRN_GROUPS * GROUP, ATTN_GROUPS * GROUP
    return pl.pallas_call(
        _norm_proj_kernel,
        grid=(t // tm,),
        in_specs=[pl.BlockSpec((tm, d), lambda i: (i, 0)),
                  _resident((1, d)),
                  _resident((d, n_h + n_a))],
        out_specs=[pl.BlockSpec((tm, n_h), lambda i: (i, 0)),
                   pl.BlockSpec((tm, n_a), lambda i: (i, 0))],
        out_shape=[jax.ShapeDtypeStruct((t, n_h), F32),
                   jax.ShapeDtypeStruct((t, n_a), BF16)],
        compiler_params=pltpu.CompilerParams(
            dimension_semantics=("arbitrary",), vmem_limit_bytes=VMEM_LIMIT),
        name="norm_proj",
    )(x2d, gain, w_in)


def _hgrn_kernel(layer, q_ref, f_ref, i_ref, g_ref, lbp_ref, ng_ref, o_ref, st_ref):
    seq = q_ref.shape[1]
    lane = lax.broadcasted_iota(jnp.int32, (1, LANES), 1)
    first_head = lane < HEAD_DIM

    lbp = lbp_ref[...]
    ex = jnp.exp(lbp - jnp.max(lbp, axis=0, keepdims=True))
    sm = ex / jnp.sum(ex, axis=0, keepdims=True)
    lb = jnp.sum(sm[:layer + 1], axis=0, keepdims=True)

    r64 = lax.broadcasted_iota(jnp.int32, (CHUNK, CHUNK), 0)
    c64 = lax.broadcasted_iota(jnp.int32, (CHUNK, CHUNK), 1)
    cum = jnp.where(c64 <= r64, 1.0, 0.0).astype(BF16)
    row = lax.broadcasted_iota(jnp.int32, (CHUNK, LANES), 0)
    col = lax.broadcasted_iota(jnp.int32, (CHUNK, LANES), 1)
    causal = (col & (HEAD_DIM - 1)) <= row
    r128 = lax.broadcasted_iota(jnp.int32, (LANES, LANES), 0)
    c128 = lax.broadcasted_iota(jnp.int32, (LANES, LANES), 1)
    same_head = (r128 < HEAD_DIM) == (c128 < HEAD_DIM)

    @pl.when(pl.program_id(1) == 0)
    def _reset_state():
        st_ref[...] = jnp.zeros_like(st_ref)

    def split_heads_rows(a):
        zero = jnp.zeros_like(a)
        return jnp.concatenate([jnp.where(first_head, a, zero),
                                jnp.where(first_head, zero, a)], axis=0)

    def chunk(n, carry):
        rows = pl.ds(pl.multiple_of(n * CHUNK, CHUNK), CHUNK)
        f = lb + (1.0 - lb) * jax.nn.sigmoid(f_ref[0, rows, :])
        logf = jnp.log(f)
        hi = logf.astype(BF16)
        lo = (logf - hi.astype(F32)).astype(BF16)
        b = (jnp.dot(cum, hi, preferred_element_type=F32)
             + jnp.dot(cum, lo, preferred_element_type=F32))
        b_end = b[CHUNK - 1:CHUNK, :]
        k = 1.0 - f
        qe_all = (q_ref[0, rows, :] * jnp.exp(b)).astype(BF16)
        ke_all = (k * jnp.exp(-b)).astype(BF16)
        ke_end_all = (k * jnp.exp(b_end - b)).astype(BF16)
        v_all = i_ref[0, rows, :].astype(BF16)
        decay = jnp.exp(b_end)
        gate = g_ref[0, rows, :]
        out_gain = ng_ref[...] * (gate * jax.nn.sigmoid(gate))

        for p in range(PAIRS):
            cols = slice(p * LANES, (p + 1) * LANES)
            qe, ke, ke_end, v = qe_all[:, cols], ke_all[:, cols], ke_end_all[:, cols], v_all[:, cols]
            scores = lax.dot_general(qe, split_heads_rows(ke), _NT, preferred_element_type=F32)
            scores = jnp.where(causal, scores, 0.0).astype(BF16)
            o = jnp.dot(scores, split_heads_rows(v), preferred_element_type=F32)
            st = st_ref[p]
            o = o + lax.dot_general(qe, st.astype(BF16), _NT, preferred_element_type=F32)
            kv_t = lax.dot_general(v, ke_end, _TN, preferred_element_type=F32)
            st_ref[p] = st * decay[:, cols] + jnp.where(same_head, kv_t, 0.0)
            o_ref[0, rows, cols] = (_pair_rms(o, first_head) * out_gain[:, cols]).astype(o_ref.dtype)
        return carry

    lax.fori_loop(0, seq // CHUNK, chunk, 0, unroll=4)


def _hgrn(proj_h, lower_bounds, norm_g, layer):
    b, s, _ = proj_h.shape
    n_lb = lower_bounds.shape[0]

    def group(c):
        return pl.BlockSpec((1, HGRN_SEQ_BLOCK, GROUP), lambda bi, si, c=c: (bi, si, c))

    return pl.pallas_call(
        functools.partial(_hgrn_kernel, layer),
        grid=(b, s // HGRN_SEQ_BLOCK),
        in_specs=[group(0), group(1), group(2), group(3),
                  _resident((n_lb, GROUP)), _resident((1, GROUP))],
        out_specs=pl.BlockSpec((1, HGRN_SEQ_BLOCK, GROUP), lambda bi, si: (bi, si, 0)),
        out_shape=jax.ShapeDtypeStruct((b, s, GROUP), BF16),
        scratch_shapes=[pltpu.VMEM((PAIRS, LANES, LANES), F32)],
        compiler_params=pltpu.CompilerParams(
            dimension_semantics=("arbitrary", "arbitrary"), vmem_limit_bytes=VMEM_LIMIT),
        name="hgrn2",
    )(proj_h, proj_h, proj_h, proj_h, lower_bounds, norm_g)


def _attn_kernel(q_ref, k_ref, v_ref, tri_ref, ng_ref, o_ref, kk_ref, vv_ref):
    qi = pl.program_id(1)
    n_blocks = k_ref.shape[1] // KEY_BLOCK
    diag_blocks = Q_BLOCK // KEY_BLOCK
    lane = lax.broadcasted_iota(jnp.int32, (1, LANES), 1)
    first_head = lane < HEAD_DIM

    @pl.when(qi == 0)
    def _split_heads():
        def body(j, carry):
            rows = pl.ds(pl.multiple_of(j * KEY_BLOCK, KEY_BLOCK), KEY_BLOCK)
            for p in range(PAIRS):
                kb = k_ref[0, rows, p * LANES:(p + 1) * LANES]
                vb = v_ref[0, rows, p * LANES:(p + 1) * LANES]
                zero = jnp.zeros_like(kb)
                kk_ref[p, j, 0:KEY_BLOCK, :] = jnp.where(first_head, kb, zero)
                kk_ref[p, j, KEY_BLOCK:2 * KEY_BLOCK, :] = jnp.where(first_head, zero, kb)
                vv_ref[p, j, 0:KEY_BLOCK, :] = jnp.where(first_head, vb, zero)
                vv_ref[p, j, KEY_BLOCK:2 * KEY_BLOCK, :] = jnp.where(first_head, zero, vb)
            return carry
        lax.fori_loop(0, n_blocks, body, 0)

    def block(j, carry, diag):
        r0 = 0 if diag is None else diag * KEY_BLOCK
        rows = Q_BLOCK - r0
        if diag is not None:
            row = lax.broadcasted_iota(jnp.int32, (rows, 2 * KEY_BLOCK), 0)
            col = lax.broadcasted_iota(jnp.int32, (rows, 2 * KEY_BLOCK), 1)
            strictly_before = (col & (KEY_BLOCK - 1)) < row
        out = []
        for p in range(PAIRS):
            run, acc = carry[p]
            q = q_ref[0, r0:, p * LANES:(p + 1) * LANES]
            z = lax.dot_general(q, kk_ref[p, j], _NT, preferred_element_type=F32)
            pos = jnp.maximum(z, 0.0)
            neg = z - pos
            log_term = jnp.log(1.0 + jnp.exp2(neg - pos)) * LOG2E
            sp = pos + log_term
            c = sp if diag is None else jnp.where(strictly_before, sp, 0.0)
            later = jnp.dot(c.astype(BF16), tri_ref[...], preferred_element_type=F32)
            pr = jnp.exp2(neg - log_term - (later + run[r0:]))
            if diag is not None:
                pr = jnp.where(strictly_before, pr, 0.0)
            pv = jnp.dot(pr.astype(BF16), vv_ref[p, j], preferred_element_type=F32)
            tot0 = later[:, 0:1] + c[:, 0:1]
            tot1 = later[:, KEY_BLOCK:KEY_BLOCK + 1] + c[:, KEY_BLOCK:KEY_BLOCK + 1]
            tot = jnp.concatenate([jnp.broadcast_to(tot0, (rows, KEY_BLOCK)),
                                   jnp.broadcast_to(tot1, (rows, KEY_BLOCK))], axis=1)
            if r0:
                run = jnp.concatenate([run[:r0], run[r0:] + tot], axis=0)
                acc = jnp.concatenate([acc[:r0], acc[r0:] + pv], axis=0)
            else:
                run, acc = run + tot, acc + pv
            out.append((run, acc))
        return tuple(out)

    carry = tuple((jnp.zeros((Q_BLOCK, 2 * KEY_BLOCK), F32), jnp.zeros((Q_BLOCK, LANES), F32))
                  for _ in range(PAIRS))
    first_key_block = qi * diag_blocks
    for diag in reversed(range(diag_blocks)):
        carry = block(first_key_block + diag, carry, diag)

    trips = first_key_block // KEY_BLOCKS_PER_ITER

    def more_blocks(state):
        t, min_run, _ = state
        return jnp.logical_and(t < trips, min_run < RUN_CUTOFF)

    def off_diagonal(state):
        t, _, c = state
        j = first_key_block - 1 - t * KEY_BLOCKS_PER_ITER
        for u in range(KEY_BLOCKS_PER_ITER):
            c = block(j - u, c, None)
        min_run = functools.reduce(jnp.minimum, [jnp.min(c[p][0]) for p in range(PAIRS)])
        return t + 1, min_run, c

    _, _, carry = lax.while_loop(more_blocks, off_diagonal, (jnp.int32(0), jnp.float32(0.0), carry))
    for p in range(PAIRS):
        cols = slice(p * LANES, (p + 1) * LANES)
        o_ref[0, :, cols] = (_pair_rms(carry[p][1], first_head) * ng_ref[:, cols]).astype(o_ref.dtype)


def _attention(proj_a, tri, norm_g):
    b, s, _ = proj_a.shape
    n_blocks = s // KEY_BLOCK
    return pl.pallas_call(
        _attn_kernel,
        grid=(b, s // Q_BLOCK),
        in_specs=[pl.BlockSpec((1, Q_BLOCK, GROUP), lambda bi, i: (bi, i, 0)),
                  pl.BlockSpec((1, s, GROUP), lambda bi, i: (bi, 0, 1)),
                  pl.BlockSpec((1, s, GROUP), lambda bi, i: (bi, 0, 2)),
                  _resident((2 * KEY_BLOCK, 2 * KEY_BLOCK)),
                  _resident((1, GROUP))],
        out_specs=pl.BlockSpec((1, Q_BLOCK, GROUP), lambda bi, i: (bi, i, 0)),
        out_shape=jax.ShapeDtypeStruct((b, s, GROUP), BF16),
        scratch_shapes=[pltpu.VMEM((PAIRS, n_blocks, 2 * KEY_BLOCK, LANES), BF16),
                        pltpu.VMEM((PAIRS, n_blocks, 2 * KEY_BLOCK, LANES), BF16)],
        compiler_params=pltpu.CompilerParams(
            dimension_semantics=("arbitrary", "arbitrary"), vmem_limit_bytes=VMEM_LIMIT),
        name="stickbreak_attn",
    )(proj_a, proj_a, proj_a, tri, norm_g)


def _suffix_sum_matrix():
    r = jnp.arange(2 * KEY_BLOCK)[:, None]
    c = jnp.arange(2 * KEY_BLOCK)[None, :]
    return ((r > c) & ((r < KEY_BLOCK) == (c < KEY_BLOCK))).astype(BF16)


def _out_ffn_kernel(final, x_ref, ma_ref, mb_ref, wo_ref, g2_ref, wg_ref, wu_ref, wd_ref,
                    g3_ref, o_ref, ff_ref):
    x1 = (x_ref[...]
          + jnp.dot(ma_ref[...], wo_ref[0:GROUP, :], preferred_element_type=F32)
          + jnp.dot(mb_ref[...], wo_ref[GROUP:2 * GROUP, :], preferred_element_type=F32))
    h = _rms(x1, g2_ref[...]).astype(BF16)
    d_ff = wg_ref.shape[1]
    for c in range(d_ff // FF_CHUNK):
        cols = slice(c * FF_CHUNK, (c + 1) * FF_CHUNK)
        gate = jnp.dot(h, wg_ref[:, cols], preferred_element_type=F32)
        up = jnp.dot(h, wu_ref[:, cols], preferred_element_type=F32)
        ff_ref[:, cols] = (gate * jax.nn.sigmoid(gate) * up).astype(BF16)
    x2 = x1 + jnp.dot(ff_ref[...], wd_ref[...], preferred_element_type=F32)
    o_ref[...] = _rms(x2, g3_ref[...]) if final else x2


def _out_ffn(x2d, mix_a, mix_b, w_out, g2, w_gate, w_up, w_down, g3, final, tm=512):
    t, d = x2d.shape
    d_ff = w_gate.shape[1]
    row = lambda n: pl.BlockSpec((tm, n), lambda i: (i, 0))
    return pl.pallas_call(
        functools.partial(_out_ffn_kernel, final),
        grid=(t // tm,),
        in_specs=[row(d), row(GROUP), row(GROUP),
                  _resident((2 * GROUP, d)), _resident((1, d)),
                  _resident((d, d_ff)), _resident((d, d_ff)), _resident((d_ff, d)),
                  _resident((1, d))],
        out_specs=row(d),
        out_shape=jax.ShapeDtypeStruct((t, d), F32),
        scratch_shapes=[pltpu.VMEM((tm, d_ff), BF16)],
        compiler_params=pltpu.CompilerParams(
            dimension_semantics=("arbitrary",), vmem_limit_bytes=VMEM_LIMIT),
        name="out_ffn",
    )(x2d, mix_a, mix_b, w_out, g2, w_gate, w_up, w_down, g3)


def kernel(x, mix_norm_g, w_in, lower_bounds, hgrn_norm_g, sb_norm_g, w_out, ffn_norm_g,
           w_gate, w_up, w_down, final_norm_g):
    b, s, d = x.shape
    depth = w_in.shape[0]
    assert w_in.shape[2] == (HGRN_GROUPS + ATTN_GROUPS) * GROUP
    assert w_gate.shape[2] % FF_CHUNK == 0 and s % KEY_BLOCK == 0
    tri = _suffix_sum_matrix()
    x2d = x.reshape(b * s, d)
    for l in range(depth):
        proj_h, proj_a = _norm_proj(x2d, mix_norm_g[l][None], w_in[l].astype(BF16))
        mix_a = _hgrn(proj_h.reshape(b, s, -1), lower_bounds, hgrn_norm_g[l][None], l)
        mix_b = _attention(proj_a.reshape(b, s, -1), tri, sb_norm_g[l][None])
        last = l == depth - 1
        g3 = final_norm_g[None] if last else ffn_norm_g[l][None]
        x2d = _out_ffn(x2d, mix_a.reshape(b * s, -1), mix_b.reshape(b * s, -1),
                       w_out[l].astype(BF16), ffn_norm_g[l][None], w_gate[l].astype(BF16),
                       w_up[l].astype(BF16), w_down[l].astype(BF16), g3, last)
    if depth == 0:
        raise ValueError("depth must be positive")
    return x2d.reshape(b, s, d)
```

```python
import functools

import jax
import jax.numpy as jnp
from jax import lax
from jax.experimental import pallas as pl
from jax.experimental.pallas import tpu as pltpu

F32 = jnp.float32
BF16 = jnp.bfloat16

LANES = 128
FF_CHUNK = 256
HEAD_DIM = 64
HEADS_PER_GROUP = 8
GROUP = HEADS_PER_GROUP * HEAD_DIM
PAIRS = GROUP // LANES
HGRN_GROUPS = 4
ATTN_GROUPS = 3
CHUNK = 64
HGRN_SEQ_BLOCK = 512
KEY_BLOCK = 128
Q_BLOCK = 512
KEY_BLOCKS_PER_ITER = 2
RUN_CUTOFF = 95.0
EPS = 1e-6
VMEM_LIMIT = 56 * 1024 * 1024

_NT = (((1,), (1,)), ((), ()))
_TN = (((0,), (0,)), ((), ()))


def _resident(shape):
    return pl.BlockSpec(shape, lambda *_: (0,) * len(shape), pipeline_mode=pl.Buffered(1))


def _rms(x, gain):
    return x * lax.rsqrt(jnp.mean(x * x, axis=-1, keepdims=True) + EPS) * gain


def _pair_rms(o, first_head):
    sq = o * o
    s0 = jnp.sum(jnp.where(first_head, sq, 0.0), axis=-1, keepdims=True)
    s1 = jnp.sum(jnp.where(first_head, 0.0, sq), axis=-1, keepdims=True)
    ms = jnp.where(first_head, s0, s1) * (1.0 / HEAD_DIM)
    return o * lax.rsqrt(ms + EPS)


def _norm_proj_kernel(x_ref, g_ref, w_ref, ph_ref, pa_ref):
    h = _rms(x_ref[...], g_ref[...]).astype(BF16)
    for c in range(HGRN_GROUPS):
        cols = slice(c * GROUP, (c + 1) * GROUP)
        ph_ref[:, cols] = jnp.dot(h, w_ref[:, cols], preferred_element_type=F32)
    base = HGRN_GROUPS * GROUP
    for c in range(ATTN_GROUPS):
        r = jnp.dot(h, w_ref[:, base + c * GROUP: base + (c + 1) * GROUP],
                    preferred_element_type=F32)
        if c == 0:
            r = r * (1.0 / HEAD_DIM ** 0.5)
        pa_ref[:, c * GROUP:(c + 1) * GROUP] = r.astype(BF16)


def _norm_proj(x2d, gain, w_in, tm=512):
    t, d = x2d.shape
    n_h, n_a = HGRN_GROUPS * GROUP, ATTN_GROUPS * GROUP
    return pl.pallas_call(
        _norm_proj_kernel,
        grid=(t // tm,),
        in_specs=[pl.BlockSpec((tm, d), lambda i: (i, 0)),
                  _resident((1, d)),
                  _resident((d, n_h + n_a))],
        out_specs=[pl.BlockSpec((tm, n_h), lambda i: (i, 0)),
                   pl.BlockSpec((tm, n_a), lambda i: (i, 0))],
        out_shape=[jax.ShapeDtypeStruct((t, n_h), F32),
                   jax.ShapeDtypeStruct((t, n_a), BF16)],
        compiler_params=pltpu.CompilerParams(
            dimension_semantics=("arbitrary",), vmem_limit_bytes=VMEM_LIMIT),
        name="norm_proj",
    )(x2d, gain, w_in)


def _hgrn_kernel(layer, q_ref, f_ref, i_ref, g_ref, lbp_ref, ng_ref, o_ref, st_ref):
    seq = q_ref.shape[1]
    lane = lax.broadcasted_iota(jnp.int32, (1, LANES), 1)
    first_head = lane < HEAD_DIM

    lbp = lbp_ref[...]
    ex = jnp.exp(lbp - jnp.max(lbp, axis=0, keepdims=True))
    sm = ex / jnp.sum(ex, axis=0, keepdims=True)
    lb = jnp.sum(sm[:layer + 1], axis=0, keepdims=True)

    r64 = lax.broadcasted_iota(jnp.int32, (CHUNK, CHUNK), 0)
    c64 = lax.broadcasted_iota(jnp.int32, (CHUNK, CHUNK), 1)
    cum = jnp.where(c64 <= r64, 1.0, 0.0).astype(BF16)
    row = lax.broadcasted_iota(jnp.int32, (CHUNK, LANES), 0)
    col = lax.broadcasted_iota(jnp.int32, (CHUNK, LANES), 1)
    causal = (col & (HEAD_DIM - 1)) <= row
    r128 = lax.broadcasted_iota(jnp.int32, (LANES, LANES), 0)
    c128 = lax.broadcasted_iota(jnp.int32, (LANES, LANES), 1)
    same_head = (r128 < HEAD_DIM) == (c128 < HEAD_DIM)

    @pl.when(pl.program_id(1) == 0)
    def _reset_state():
        st_ref[...] = jnp.zeros_like(st_ref)

    def split_heads_rows(a):
        zero = jnp.zeros_like(a)
        return jnp.concatenate([jnp.where(first_head, a, zero),
                                jnp.where(first_head, zero, a)], axis=0)

    def chunk(n, carry):
        rows = pl.ds(pl.multiple_of(n * CHUNK, CHUNK), CHUNK)
        f = lb + (1.0 - lb) * jax.nn.sigmoid(f_ref[0, rows, :])
        logf = jnp.log(f)
        hi = logf.astype(BF16)
        lo = (logf - hi.astype(F32)).astype(BF16)
        b = (jnp.dot(cum, hi, preferred_element_type=F32)
             + jnp.dot(cum, lo, preferred_element_type=F32))
        b_end = b[CHUNK - 1:CHUNK, :]
        k = 1.0 - f
        qe_all = (q_ref[0, rows, :] * jnp.exp(b)).astype(BF16)
        ke_all = (k * jnp.exp(-b)).astype(BF16)
        ke_end_all = (k * jnp.exp(b_end - b)).astype(BF16)
        v_all = i_ref[0, rows, :].astype(BF16)
        decay = jnp.exp(b_end)
        gate = g_ref[0, rows, :]
        out_gain = ng_ref[...] * (gate * jax.nn.sigmoid(gate))

        for p in range(PAIRS):
            cols = slice(p * LANES, (p + 1) * LANES)
            qe, ke, ke_end, v = qe_all[:, cols], ke_all[:, cols], ke_end_all[:, cols], v_all[:, cols]
            scores = lax.dot_general(qe, split_heads_rows(ke), _NT, preferred_element_type=F32)
            scores = jnp.where(causal, scores, 0.0).astype(BF16)
            o = jnp.dot(scores, split_heads_rows(v), preferred_element_type=F32)
            st = st_ref[p]
            o = o + lax.dot_general(qe, st.astype(BF16), _NT, preferred_element_type=F32)
            kv_t = lax.dot_general(v, ke_end, _TN, preferred_element_type=F32)
            st_ref[p] = st * decay[:, cols] + jnp.where(same_head, kv_t, 0.0)
            o_ref[0, rows, cols] = (_pair_rms(o, first_head) * out_gain[:, cols]).astype(o_ref.dtype)
        return carry

    lax.fori_loop(0, seq // CHUNK, chunk, 0, unroll=4)


def _hgrn(proj_h, lower_bounds, norm_g, layer):
    b, s, _ = proj_h.shape
    n_lb = lower_bounds.shape[0]

    def group(c):
        return pl.BlockSpec((1, HGRN_SEQ_BLOCK, GROUP), lambda bi, si, c=c: (bi, si, c))

    return pl.pallas_call(
        functools.partial(_hgrn_kernel, layer),
        grid=(b, s // HGRN_SEQ_BLOCK),
        in_specs=[group(0), group(1), group(2), group(3),
                  _resident((n_lb, GROUP)), _resident((1, GROUP))],
        out_specs=pl.BlockSpec((1, HGRN_SEQ_BLOCK, GROUP), lambda bi, si: (bi, si, 0)),
        out_shape=jax.ShapeDtypeStruct((b, s, GROUP), BF16),
        scratch_shapes=[pltpu.VMEM((PAIRS, LANES, LANES), F32)],
        compiler_params=pltpu.CompilerParams(
            dimension_semantics=("arbitrary", "arbitrary"), vmem_limit_bytes=VMEM_LIMIT),
        name="hgrn2",
    )(proj_h, proj_h, proj_h, proj_h, lower_bounds, norm_g)


def _attn_kernel(q_ref, k_ref, v_ref, tri_ref, ng_ref, o_ref, kk_ref, vv_ref):
    qi = pl.program_id(1)
    n_blocks = k_ref.shape[1] // KEY_BLOCK
    diag_blocks = Q_BLOCK // KEY_BLOCK
    lane = lax.broadcasted_iota(jnp.int32, (1, LANES), 1)
    first_head = lane < HEAD_DIM

    @pl.when(qi == 0)
    def _split_heads():
        dim_first_head = lax.broadcasted_iota(jnp.int32, (LANES, KEY_BLOCK), 0) < HEAD_DIM

        def body(j, carry):
            rows = pl.ds(pl.multiple_of(j * KEY_BLOCK, KEY_BLOCK), KEY_BLOCK)
            for p in range(PAIRS):
                kt = k_ref[0, rows, p * LANES:(p + 1) * LANES].astype(F32).T
                vb = v_ref[0, rows, p * LANES:(p + 1) * LANES]
                zero = jnp.zeros_like(vb)
                kk_ref[p, j, :, 0:KEY_BLOCK] = jnp.where(dim_first_head, kt, 0.0).astype(BF16)
                kk_ref[p, j, :, KEY_BLOCK:2 * KEY_BLOCK] = jnp.where(dim_first_head, 0.0, kt).astype(BF16)
                vv_ref[p, j, 0:KEY_BLOCK, :] = jnp.where(first_head, vb, zero)
                vv_ref[p, j, KEY_BLOCK:2 * KEY_BLOCK, :] = jnp.where(first_head, zero, vb)
            return carry
        lax.fori_loop(0, n_blocks, body, 0)

    def block(j, carry, diag):
        r0 = 0 if diag is None else diag * KEY_BLOCK
        rows = Q_BLOCK - r0
        if diag is not None:
            row = lax.broadcasted_iota(jnp.int32, (KEY_BLOCK, 2 * KEY_BLOCK), 0)
            col = lax.broadcasted_iota(jnp.int32, (KEY_BLOCK, 2 * KEY_BLOCK), 1)
            strictly_before = (col & (KEY_BLOCK - 1)) < row

        def causal(a):
            if diag is None:
                return a
            top = jnp.where(strictly_before, a[:KEY_BLOCK], 0.0)
            return top if rows == KEY_BLOCK else jnp.concatenate([top, a[KEY_BLOCK:]], axis=0)

        zs, cs = [], []
        for p in range(PAIRS):
            q = q_ref[0, r0:, p * LANES:(p + 1) * LANES]
            z = jnp.dot(q, kk_ref[p, j], preferred_element_type=F32)
            zb = z.astype(BF16)
            pos = jnp.maximum(zb, 0.0)
            neg = zb - pos
            c = causal(pos + jnp.log(1.0 + jnp.exp(neg - pos)))
            zs.append(z)
            cs.append(c)
        incl_all = jnp.dot(jnp.concatenate(cs, axis=0), tri_ref[...], preferred_element_type=F32)

        out = []
        for p in range(PAIRS):
            run, acc = carry[p]
            incl = incl_all[p * rows:(p + 1) * rows]
            pr = causal(jnp.exp(zs[p] - incl - run[r0:]))
            pv = jnp.dot(pr.astype(BF16), vv_ref[p, j], preferred_element_type=F32)
            tot = jnp.concatenate(
                [jnp.broadcast_to(incl[:, 0:1], (rows, KEY_BLOCK)),
                 jnp.broadcast_to(incl[:, KEY_BLOCK:KEY_BLOCK + 1], (rows, KEY_BLOCK))], axis=1)
            if r0:
                run = jnp.concatenate([run[:r0], run[r0:] + tot], axis=0)
                acc = jnp.concatenate([acc[:r0], acc[r0:] + pv], axis=0)
            else:
                run, acc = run + tot, acc + pv
            out.append((run, acc))
        return tuple(out)

    carry = tuple((jnp.zeros((Q_BLOCK, 2 * KEY_BLOCK), F32), jnp.zeros((Q_BLOCK, LANES), F32))
                  for _ in range(PAIRS))
    first_key_block = qi * diag_blocks
    for diag in reversed(range(diag_blocks)):
        carry = block(first_key_block + diag, carry, diag)

    trips = first_key_block // KEY_BLOCKS_PER_ITER

    def more_blocks(state):
        t, min_run, _ = state
        return jnp.logical_and(t < trips, min_run < RUN_CUTOFF)

    def off_diagonal(state):
        t, _, c = state
        j = first_key_block - 1 - t * KEY_BLOCKS_PER_ITER
        for u in range(KEY_BLOCKS_PER_ITER):
            c = block(j - u, c, None)
        min_run = functools.reduce(jnp.minimum, [jnp.min(c[p][0]) for p in range(PAIRS)])
        return t + 1, min_run, c

    _, _, carry = lax.while_loop(more_blocks, off_diagonal, (jnp.int32(0), jnp.float32(0.0), carry))
    for p in range(PAIRS):
        cols = slice(p * LANES, (p + 1) * LANES)
        o_ref[0, :, cols] = (_pair_rms(carry[p][1], first_head) * ng_ref[:, cols]).astype(o_ref.dtype)


def _attention(proj_a, tri, norm_g):
    b, s, _ = proj_a.shape
    n_blocks = s // KEY_BLOCK
    return pl.pallas_call(
        _attn_kernel,
        grid=(b, s // Q_BLOCK),
        in_specs=[pl.BlockSpec((1, Q_BLOCK, GROUP), lambda bi, i: (bi, i, 0)),
                  pl.BlockSpec((1, s, GROUP), lambda bi, i: (bi, 0, 1)),
                  pl.BlockSpec((1, s, GROUP), lambda bi, i: (bi, 0, 2)),
                  _resident((2 * KEY_BLOCK, 2 * KEY_BLOCK)),
                  _resident((1, GROUP))],
        out_specs=pl.BlockSpec((1, Q_BLOCK, GROUP), lambda bi, i: (bi, i, 0)),
        out_shape=jax.ShapeDtypeStruct((b, s, GROUP), BF16),
        scratch_shapes=[pltpu.VMEM((PAIRS, n_blocks, LANES, 2 * KEY_BLOCK), BF16),
                        pltpu.VMEM((PAIRS, n_blocks, 2 * KEY_BLOCK, LANES), BF16)],
        compiler_params=pltpu.CompilerParams(
            dimension_semantics=("arbitrary", "arbitrary"), vmem_limit_bytes=VMEM_LIMIT),
        name="stickbreak_attn",
    )(proj_a, proj_a, proj_a, tri, norm_g)


def _suffix_sum_matrix():
    r = jnp.arange(2 * KEY_BLOCK)[:, None]
    c = jnp.arange(2 * KEY_BLOCK)[None, :]
    return ((r >= c) & ((r < KEY_BLOCK) == (c < KEY_BLOCK))).astype(BF16)


def _out_ffn_kernel(final, x_ref, ma_ref, mb_ref, wo_ref, g2_ref, wg_ref, wu_ref, wd_ref,
                    g3_ref, o_ref, ff_ref):
    x1 = (x_ref[...]
          + jnp.dot(ma_ref[...], wo_ref[0:GROUP, :], preferred_element_type=F32)
          + jnp.dot(mb_ref[...], wo_ref[GROUP:2 * GROUP, :], preferred_element_type=F32))
    h = _rms(x1, g2_ref[...]).astype(BF16)
    d_ff = wg_ref.shape[1]
    for c in range(d_ff // FF_CHUNK):
        cols = slice(c * FF_CHUNK, (c + 1) * FF_CHUNK)
        gate = jnp.dot(h, wg_ref[:, cols], preferred_element_type=F32)
        up = jnp.dot(h, wu_ref[:, cols], preferred_element_type=F32)
        ff_ref[:, cols] = (gate * jax.nn.sigmoid(gate) * up).astype(BF16)
    x2 = x1 + jnp.dot(ff_ref[...], wd_ref[...], preferred_element_type=F32)
    o_ref[...] = _rms(x2, g3_ref[...]) if final else x2


def _out_ffn(x2d, mix_a, mix_b, w_out, g2, w_gate, w_up, w_down, g3, final, tm=512):
    t, d = x2d.shape
    d_ff = w_gate.shape[1]
    row = lambda n: pl.BlockSpec((tm, n), lambda i: (i, 0))
    return pl.pallas_call(
        functools.partial(_out_ffn_kernel, final),
        grid=(t // tm,),
        in_specs=[row(d), row(GROUP), row(GROUP),
                  _resident((2 * GROUP, d)), _resident((1, d)),
                  _resident((d, d_ff)), _resident((d, d_ff)), _resident((d_ff, d)),
                  _resident((1, d))],
        out_specs=row(d),
        out_shape=jax.ShapeDtypeStruct((t, d), F32),
        scratch_shapes=[pltpu.VMEM((tm, d_ff), BF16)],
        compiler_params=pltpu.CompilerParams(
            dimension_semantics=("arbitrary",), vmem_limit_bytes=VMEM_LIMIT),
        name="out_ffn",
    )(x2d, mix_a, mix_b, w_out, g2, w_gate, w_up, w_down, g3)


def kernel(x, mix_norm_g, w_in, lower_bounds, hgrn_norm_g, sb_norm_g, w_out, ffn_norm_g,
           w_gate, w_up, w_down, final_norm_g):
    b, s, d = x.shape
    depth = w_in.shape[0]
    assert w_in.shape[2] == (HGRN_GROUPS + ATTN_GROUPS) * GROUP
    assert w_gate.shape[2] % FF_CHUNK == 0 and s % KEY_BLOCK == 0
    tri = _suffix_sum_matrix()
    x2d = x.reshape(b * s, d)
    for l in range(depth):
        proj_h, proj_a = _norm_proj(x2d, mix_norm_g[l][None], w_in[l].astype(BF16))
        mix_a = _hgrn(proj_h.reshape(b, s, -1), lower_bounds, hgrn_norm_g[l][None], l)
        mix_b = _attention(proj_a.reshape(b, s, -1), tri, sb_norm_g[l][None])
        last = l == depth - 1
        g3 = final_norm_g[None] if last else ffn_norm_g[l][None]
        x2d = _out_ffn(x2d, mix_a.reshape(b * s, -1), mix_b.reshape(b * s, -1),
                       w_out[l].astype(BF16), ffn_norm_g[l][None], w_gate[l].astype(BF16),
                       w_up[l].astype(BF16), w_down[l].astype(BF16), g3, last)
    if depth == 0:
        raise ValueError("depth must be positive")
    return x2d.reshape(b, s, d)
```

```python
import functools

import jax
import jax.numpy as jnp
from jax import lax
from jax.experimental import pallas as pl
from jax.experimental.pallas import tpu as pltpu

F32 = jnp.float32
BF16 = jnp.bfloat16

LANES = 128
FF_CHUNK = 256
HEAD_DIM = 64
HEADS_PER_GROUP = 8
GROUP = HEADS_PER_GROUP * HEAD_DIM
PAIRS = GROUP // LANES
HGRN_GROUPS = 4
ATTN_GROUPS = 3
CHUNK = 64
ROW_TILE = 512
KEY_BLOCK = 128
Q_BLOCK = 512
DIAG_ALWAYS_TILES = 3
RUN_CUTOFF = 95.0
EPS = 1e-6
VMEM_LIMIT = 56 * 1024 * 1024

_NT = (((1,), (1,)), ((), ()))
_TN = (((0,), (0,)), ((), ()))


def _resident(shape):
    return pl.BlockSpec(shape, lambda *_: (0,) * len(shape), pipeline_mode=pl.Buffered(1))


def _rms(x, gain):
    return x * lax.rsqrt(jnp.mean(x * x, axis=-1, keepdims=True) + EPS) * gain


def _pair_rms(o, first_head):
    sq = o * o
    s0 = jnp.sum(jnp.where(first_head, sq, 0.0), axis=-1, keepdims=True)
    s1 = jnp.sum(jnp.where(first_head, 0.0, sq), axis=-1, keepdims=True)
    ms = jnp.where(first_head, s0, s1) * (1.0 / HEAD_DIM)
    return o * lax.rsqrt(ms + EPS)


def _project_steps(x_ref, g_ref, w_ref, rows, ph_ref, pa_ref):
    h = []

    def hgrn_group(c):
        if not h:
            h.append(_rms(x_ref[rows, :], g_ref[...]).astype(BF16))
        cols = slice(c * GROUP, (c + 1) * GROUP)
        ph_ref[:, cols] = jnp.dot(h[0], w_ref[:, cols], preferred_element_type=F32)

    def attn_group(c):
        base = HGRN_GROUPS * GROUP
        r = jnp.dot(h[0], w_ref[:, base + c * GROUP: base + (c + 1) * GROUP],
                    preferred_element_type=F32)
        if c == 0:
            r = r * (1.0 / HEAD_DIM ** 0.5)
        pa_ref[rows, c * GROUP:(c + 1) * GROUP] = r.astype(BF16)

    return ([functools.partial(hgrn_group, c) for c in range(HGRN_GROUPS)]
            + [functools.partial(attn_group, c) for c in range(ATTN_GROUPS)])


def _interleave(first, second):
    for k in range(max(len(first), len(second))):
        if k < len(first):
            first[k]()
        if k < len(second):
            second[k]()


def _hgrn_steps(consts, ph_ref, ng_ref, st_ref, o_ref, out_row0):
    lb, cum, causal, same_head, first_head = consts

    def split_heads_rows(a):
        zero = jnp.zeros_like(a)
        return jnp.concatenate([jnp.where(first_head, a, zero),
                                jnp.where(first_head, zero, a)], axis=0)

    def group(rows, c):
        return ph_ref[rows, c * GROUP:(c + 1) * GROUP]

    def chunk(n):
        rows = slice(n * CHUNK, (n + 1) * CHUNK)
        f = lb + (1.0 - lb) * jax.nn.sigmoid(group(rows, 1))
        logf = jnp.log(f)
        hi = logf.astype(BF16)
        lo = (logf - hi.astype(F32)).astype(BF16)
        b = (jnp.dot(cum, hi, preferred_element_type=F32)
             + jnp.dot(cum, lo, preferred_element_type=F32))
        b_end = b[CHUNK - 1:CHUNK, :]
        k = 1.0 - f
        qe_all = (group(rows, 0) * jnp.exp(b)).astype(BF16)
        ke_all = (k * jnp.exp(-b)).astype(BF16)
        ke_end_all = (k * jnp.exp(b_end - b)).astype(BF16)
        v_all = group(rows, 2).astype(BF16)
        decay = jnp.exp(b_end)
        gate = group(rows, 3)
        out_gain = ng_ref[...] * (gate * jax.nn.sigmoid(gate))
        out_rows = slice(out_row0 + n * CHUNK, out_row0 + (n + 1) * CHUNK)

        for p in range(PAIRS):
            cols = slice(p * LANES, (p + 1) * LANES)
            qe, ke, ke_end, v = qe_all[:, cols], ke_all[:, cols], ke_end_all[:, cols], v_all[:, cols]
            scores = lax.dot_general(qe, split_heads_rows(ke), _NT, preferred_element_type=F32)
            scores = jnp.where(causal, scores, 0.0).astype(BF16)
            o = jnp.dot(scores, split_heads_rows(v), preferred_element_type=F32)
            st = st_ref[p]
            o = o + lax.dot_general(qe, st.astype(BF16), _NT, preferred_element_type=F32)
            kv_t = lax.dot_general(v, ke_end, _TN, preferred_element_type=F32)
            st_ref[p] = st * decay[:, cols] + jnp.where(same_head, kv_t, 0.0)
            o_ref[out_rows, cols] = (_pair_rms(o, first_head) * out_gain[:, cols]).astype(o_ref.dtype)

    return [functools.partial(chunk, n) for n in range(ph_ref.shape[0] // CHUNK)]


def _proj_hgrn_kernel(layer, tiles_per_seq, x_ref, g_ref, w_ref, lbp_ref, ng_ref,
                      pa_ref, o_ref, pha_ref, phb_ref, st_ref):
    i = pl.program_id(0)
    tm = pha_ref.shape[0]
    lane = lax.broadcasted_iota(jnp.int32, (1, LANES), 1)
    first_head = lane < HEAD_DIM

    lbp = lbp_ref[...]
    ex = jnp.exp(lbp - jnp.max(lbp, axis=0, keepdims=True))
    sm = ex / jnp.sum(ex, axis=0, keepdims=True)
    lb = jnp.sum(sm[:layer + 1], axis=0, keepdims=True)

    r64 = lax.broadcasted_iota(jnp.int32, (CHUNK, CHUNK), 0)
    c64 = lax.broadcasted_iota(jnp.int32, (CHUNK, CHUNK), 1)
    cum = jnp.where(c64 <= r64, 1.0, 0.0).astype(BF16)
    row = lax.broadcasted_iota(jnp.int32, (CHUNK, LANES), 0)
    col = lax.broadcasted_iota(jnp.int32, (CHUNK, LANES), 1)
    causal = (col & (HEAD_DIM - 1)) <= row
    r128 = lax.broadcasted_iota(jnp.int32, (LANES, LANES), 0)
    c128 = lax.broadcasted_iota(jnp.int32, (LANES, LANES), 1)
    same_head = (r128 < HEAD_DIM) == (c128 < HEAD_DIM)
    consts = (lb, cum, causal, same_head, first_head)

    @pl.when(i == 0)
    def _first_step():
        phb_ref[...] = jnp.zeros_like(phb_ref)
        st_ref[...] = jnp.zeros_like(st_ref)

    _interleave(_hgrn_steps(consts, phb_ref, ng_ref, st_ref, o_ref, 0),
                _project_steps(x_ref, g_ref, w_ref, slice(0, tm), pha_ref, pa_ref))
    keep = jnp.where(lax.rem(2 * i, tiles_per_seq) == 0, 0.0, 1.0)
    st_ref[...] = st_ref[...] * keep
    _interleave(_hgrn_steps(consts, pha_ref, ng_ref, st_ref, o_ref, tm),
                _project_steps(x_ref, g_ref, w_ref, slice(tm, 2 * tm), phb_ref, pa_ref))


def _proj_hgrn(x2d, gain, w_in, lower_bounds, norm_g, layer, seq, tm=ROW_TILE):
    t, d = x2d.shape
    n_h, n_a = HGRN_GROUPS * GROUP, ATTN_GROUPS * GROUP
    n_lb = lower_bounds.shape[0]
    tiles_per_seq = seq // tm
    assert seq % tm == 0 and tiles_per_seq % 2 == 0
    steps = t // (2 * tm)
    last = steps - 1
    return pl.pallas_call(
        functools.partial(_proj_hgrn_kernel, layer, tiles_per_seq),
        grid=(steps + 1,),
        in_specs=[pl.BlockSpec((2 * tm, d), lambda i: (jnp.minimum(i, last), 0)),
                  _resident((1, d)),
                  _resident((d, n_h + n_a)),
                  _resident((n_lb, GROUP)), _resident((1, GROUP))],
        out_specs=[pl.BlockSpec((2 * tm, n_a), lambda i: (jnp.minimum(i, last), 0)),
                   pl.BlockSpec((2 * tm, GROUP), lambda i: (i, 0))],
        out_shape=[jax.ShapeDtypeStruct((t, n_a), BF16),
                   jax.ShapeDtypeStruct((t + 2 * tm, GROUP), BF16)],
        scratch_shapes=[pltpu.VMEM((tm, n_h), F32), pltpu.VMEM((tm, n_h), F32),
                        pltpu.VMEM((PAIRS, LANES, LANES), F32)],
        compiler_params=pltpu.CompilerParams(
            dimension_semantics=("arbitrary",), vmem_limit_bytes=VMEM_LIMIT),
        name="proj_hgrn2",
    )(x2d, gain, w_in, lower_bounds, norm_g)


def _attn_kernel(q_ref, k_ref, v_ref, tri_ref, ng_ref, o_ref, kk_ref, vv_ref, run_ref, acc_ref):
    qi = pl.program_id(1)
    n_blocks = k_ref.shape[1] // KEY_BLOCK
    diag_blocks = Q_BLOCK // KEY_BLOCK
    lane = lax.broadcasted_iota(jnp.int32, (1, LANES), 1)
    first_head = lane < HEAD_DIM

    @pl.when(qi == 0)
    def _split_heads():
        dim_first_head = lax.broadcasted_iota(jnp.int32, (LANES, KEY_BLOCK), 0) < HEAD_DIM

        def body(j, carry):
            rows = pl.ds(pl.multiple_of(j * KEY_BLOCK, KEY_BLOCK), KEY_BLOCK)
            for p in range(PAIRS):
                kt = k_ref[0, rows, p * LANES:(p + 1) * LANES].astype(F32).T
                vb = v_ref[0, rows, p * LANES:(p + 1) * LANES]
                zero = jnp.zeros_like(vb)
                kk_ref[p, j, :, 0:KEY_BLOCK] = jnp.where(dim_first_head, kt, 0.0).astype(BF16)
                kk_ref[p, j, :, KEY_BLOCK:2 * KEY_BLOCK] = jnp.where(dim_first_head, 0.0, kt).astype(BF16)
                vv_ref[p, j, 0:KEY_BLOCK, :] = jnp.where(first_head, vb, zero)
                vv_ref[p, j, KEY_BLOCK:2 * KEY_BLOCK, :] = jnp.where(first_head, zero, vb)
            return carry
        lax.fori_loop(0, n_blocks, body, 0)

    def block(j, r0, r1, diagonal):
        rows = r1 - r0
        if diagonal:
            row = lax.broadcasted_iota(jnp.int32, (KEY_BLOCK, 2 * KEY_BLOCK), 0)
            col = lax.broadcasted_iota(jnp.int32, (KEY_BLOCK, 2 * KEY_BLOCK), 1)
            strictly_before = (col & (KEY_BLOCK - 1)) < row

        def causal(a):
            if not diagonal:
                return a
            top = jnp.where(strictly_before, a[:KEY_BLOCK], 0.0)
            return top if rows == KEY_BLOCK else jnp.concatenate([top, a[KEY_BLOCK:]], axis=0)

        def earlier(ref, p):
            if not diagonal:
                return ref[p, r0:r1]
            top = jnp.zeros((KEY_BLOCK, ref.shape[2]), F32)
            return top if rows == KEY_BLOCK else jnp.concatenate([top, ref[p, r0 + KEY_BLOCK:r1]], axis=0)

        zs, cs = [], []
        for p in range(PAIRS):
            q = q_ref[0, r0:r1, p * LANES:(p + 1) * LANES]
            z = jnp.dot(q, kk_ref[p, j], preferred_element_type=F32)
            zb = z.astype(BF16)
            pos = jnp.maximum(zb, 0.0)
            neg = zb - pos
            c = causal(pos + jnp.log(1.0 + jnp.exp(neg - pos)))
            zs.append(z)
            cs.append(c)
        incl_all = jnp.dot(jnp.concatenate(cs, axis=0), tri_ref[...], preferred_element_type=F32)

        for p in range(PAIRS):
            incl = incl_all[p * rows:(p + 1) * rows]
            run = earlier(run_ref, p)
            pr = causal(jnp.exp(zs[p] - incl - run))
            pv = jnp.dot(pr.astype(BF16), vv_ref[p, j], preferred_element_type=F32)
            tot = jnp.concatenate(
                [jnp.broadcast_to(incl[:, 0:1], (rows, KEY_BLOCK)),
                 jnp.broadcast_to(incl[:, KEY_BLOCK:KEY_BLOCK + 1], (rows, KEY_BLOCK))], axis=1)
            run_ref[p, r0:r1] = run + tot
            acc_ref[p, r0:r1] = earlier(acc_ref, p) + pv

    def tiles_needed(lowest=0):
        need = jnp.int32(lowest)
        for t in range(lowest, diag_blocks):
            live = functools.reduce(jnp.minimum, [
                jnp.min(run_ref[p, t * KEY_BLOCK:(t + 1) * KEY_BLOCK]) for p in range(PAIRS)])
            need = jnp.where(live < RUN_CUTOFF, t + 1, need)
        return need

    def limited_block(j, r0, need, least, diagonal):
        branches = [functools.partial(block, r0=r0, r1=n * KEY_BLOCK, diagonal=diagonal)
                    for n in range(least, diag_blocks + 1)]
        lax.switch(need - least, branches, j)

    first_key_block = qi * diag_blocks
    for diag in reversed(range(diag_blocks)):
        r0 = diag * KEY_BLOCK
        least = diag + DIAG_ALWAYS_TILES
        if least >= diag_blocks:
            block(first_key_block + diag, r0, Q_BLOCK, True)
        else:
            limited_block(first_key_block + diag, r0, tiles_needed(least), least, True)

    def more_blocks(state):
        t, need = state
        return jnp.logical_and(t < first_key_block, need > 0)

    def off_diagonal(state):
        t, need = state
        limited_block(first_key_block - 1 - t, 0, need, 1, False)
        return t + 1, tiles_needed()

    lax.while_loop(more_blocks, off_diagonal, (jnp.int32(0), tiles_needed()))
    for p in range(PAIRS):
        cols = slice(p * LANES, (p + 1) * LANES)
        o_ref[0, :, cols] = (_pair_rms(acc_ref[p], first_head) * ng_ref[:, cols]).astype(o_ref.dtype)


def _attention(proj_a, tri, norm_g):
    b, s, _ = proj_a.shape
    n_blocks = s // KEY_BLOCK
    return pl.pallas_call(
        _attn_kernel,
        grid=(b, s // Q_BLOCK),
        in_specs=[pl.BlockSpec((1, Q_BLOCK, GROUP), lambda bi, i: (bi, i, 0)),
                  pl.BlockSpec((1, s, GROUP), lambda bi, i: (bi, 0, 1)),
                  pl.BlockSpec((1, s, GROUP), lambda bi, i: (bi, 0, 2)),
                  _resident((2 * KEY_BLOCK, 2 * KEY_BLOCK)),
                  _resident((1, GROUP))],
        out_specs=pl.BlockSpec((1, Q_BLOCK, GROUP), lambda bi, i: (bi, i, 0)),
        out_shape=jax.ShapeDtypeStruct((b, s, GROUP), BF16),
        scratch_shapes=[pltpu.VMEM((PAIRS, n_blocks, LANES, 2 * KEY_BLOCK), BF16),
                        pltpu.VMEM((PAIRS, n_blocks, 2 * KEY_BLOCK, LANES), BF16),
                        pltpu.VMEM((PAIRS, Q_BLOCK, 2 * KEY_BLOCK), F32),
                        pltpu.VMEM((PAIRS, Q_BLOCK, LANES), F32)],
        compiler_params=pltpu.CompilerParams(
            dimension_semantics=("arbitrary", "arbitrary"), vmem_limit_bytes=VMEM_LIMIT),
        name="stickbreak_attn",
    )(proj_a, proj_a, proj_a, tri, norm_g)


def _suffix_sum_matrix():
    r = jnp.arange(2 * KEY_BLOCK)[:, None]
    c = jnp.arange(2 * KEY_BLOCK)[None, :]
    return ((r >= c) & ((r < KEY_BLOCK) == (c < KEY_BLOCK))).astype(BF16)


def _out_ffn_kernel(final, x_ref, ma_ref, mb_ref, wo_ref, g2_ref, wg_ref, wu_ref, wd_ref,
                    g3_ref, o_ref, ff_ref):
    x1 = (x_ref[...]
          + jnp.dot(ma_ref[...], wo_ref[0:GROUP, :], preferred_element_type=F32)
          + jnp.dot(mb_ref[...], wo_ref[GROUP:2 * GROUP, :], preferred_element_type=F32))
    h = _rms(x1, g2_ref[...]).astype(BF16)
    d_ff = wg_ref.shape[1]
    for c in range(d_ff // FF_CHUNK):
        cols = slice(c * FF_CHUNK, (c + 1) * FF_CHUNK)
        gate = jnp.dot(h, wg_ref[:, cols], preferred_element_type=F32)
        up = jnp.dot(h, wu_ref[:, cols], preferred_element_type=F32)
        ff_ref[:, cols] = (gate * jax.nn.sigmoid(gate) * up).astype(BF16)
    x2 = x1 + jnp.dot(ff_ref[...], wd_ref[...], preferred_element_type=F32)
    o_ref[...] = _rms(x2, g3_ref[...]) if final else x2


def _out_ffn(x2d, mix_a, mix_b, w_out, g2, w_gate, w_up, w_down, g3, final, tm=ROW_TILE):
    t, d = x2d.shape
    d_ff = w_gate.shape[1]
    row = lambda n: pl.BlockSpec((tm, n), lambda i: (i, 0))
    return pl.pallas_call(
        functools.partial(_out_ffn_kernel, final),
        grid=(t // tm,),
        in_specs=[row(d), pl.BlockSpec((tm, GROUP), lambda i: (i + 1, 0)), row(GROUP),
                  _resident((2 * GROUP, d)), _resident((1, d)),
                  _resident((d, d_ff)), _resident((d, d_ff)), _resident((d_ff, d)),
                  _resident((1, d))],
        out_specs=row(d),
        out_shape=jax.ShapeDtypeStruct((t, d), F32),
        scratch_shapes=[pltpu.VMEM((tm, d_ff), BF16)],
        compiler_params=pltpu.CompilerParams(
            dimension_semantics=("arbitrary",), vmem_limit_bytes=VMEM_LIMIT),
        name="out_ffn",
    )(x2d, mix_a, mix_b, w_out, g2, w_gate, w_up, w_down, g3)


def kernel(x, mix_norm_g, w_in, lower_bounds, hgrn_norm_g, sb_norm_g, w_out, ffn_norm_g,
           w_gate, w_up, w_down, final_norm_g):
    b, s, d = x.shape
    depth = w_in.shape[0]
    assert w_in.shape[2] == (HGRN_GROUPS + ATTN_GROUPS) * GROUP
    assert w_gate.shape[2] % FF_CHUNK == 0 and s % KEY_BLOCK == 0
    tri = _suffix_sum_matrix()
    x2d = x.reshape(b * s, d)
    for l in range(depth):
        proj_a, mix_a = _proj_hgrn(x2d, mix_norm_g[l][None], w_in[l].astype(BF16), lower_bounds,
                                   hgrn_norm_g[l][None], l, s)
        mix_b = _attention(proj_a.reshape(b, s, -1), tri, sb_norm_g[l][None])
        last = l == depth - 1
        g3 = final_norm_g[None] if last else ffn_norm_g[l][None]
        x2d = _out_ffn(x2d, mix_a, mix_b.reshape(b * s, -1),
                       w_out[l].astype(BF16), ffn_norm_g[l][None], w_gate[l].astype(BF16),
                       w_up[l].astype(BF16), w_down[l].astype(BF16), g3, last)
    if depth == 0:
        raise ValueError("depth must be positive")
    return x2d.reshape(b, s, d)
```

```python
import functools

import jax
import jax.numpy as jnp
from jax import lax
from jax.experimental import pallas as pl
from jax.experimental.pallas import tpu as pltpu

F32 = jnp.float32
BF16 = jnp.bfloat16

LANES = 128
FF_CHUNK = 256
PROJ_COLS = 256
HGRN_STAGE_PARTS = 2
HEAD_DIM = 64
HEADS_PER_GROUP = 8
GROUP = HEADS_PER_GROUP * HEAD_DIM
PAIRS = GROUP // LANES
HGRN_GROUPS = 4
ATTN_GROUPS = 3
CHUNK = 64
ROW_TILE = 512
KEY_BLOCK = 128
Q_BLOCK = 512
DIAG_ALWAYS_TILES = 3
RUN_CUTOFF = 95.0
EPS = 1e-6
VMEM_LIMIT = 56 * 1024 * 1024

_NT = (((1,), (1,)), ((), ()))
_TN = (((0,), (0,)), ((), ()))


def _resident(shape):
    return pl.BlockSpec(shape, lambda *_: (0,) * len(shape), pipeline_mode=pl.Buffered(1))


def _rms(x, gain):
    return x * lax.rsqrt(jnp.mean(x * x, axis=-1, keepdims=True) + EPS) * gain


def _pair_rms(o, first_head):
    sq = o * o
    s0 = jnp.sum(jnp.where(first_head, sq, 0.0), axis=-1, keepdims=True)
    s1 = jnp.sum(jnp.where(first_head, 0.0, sq), axis=-1, keepdims=True)
    ms = jnp.where(first_head, s0, s1) * (1.0 / HEAD_DIM)
    return o * lax.rsqrt(ms + EPS)


def _project_steps(x_ref, g_ref, w_ref, rows, ph_ref, pa_ref):
    h = []
    n_h = HGRN_GROUPS * GROUP

    def piece(c0):
        if not h:
            h.append(_rms(x_ref[rows, :], g_ref[...]).astype(BF16))
        r = jnp.dot(h[0], w_ref[:, c0:c0 + PROJ_COLS], preferred_element_type=F32)
        if c0 < n_h:
            ph_ref[:, c0:c0 + PROJ_COLS] = r
        else:
            if c0 - n_h < GROUP:
                r = r * (1.0 / HEAD_DIM ** 0.5)
            pa_ref[rows, c0 - n_h:c0 - n_h + PROJ_COLS] = r.astype(BF16)

    return [functools.partial(piece, c0) for c0 in range(0, w_ref.shape[1], PROJ_COLS)]


def _interleave(first, second):
    for k in range(max(len(first), len(second))):
        if k < len(first):
            first[k]()
        if k < len(second):
            second[k]()


def _hgrn_steps(consts, ph_ref, ng_ref, st_ref, o_ref, out_row0):
    lb, cum, causal, same_head, first_head = consts
    n_chunks = ph_ref.shape[0] // CHUNK
    pair_cols = [slice(p * LANES, (p + 1) * LANES) for p in range(PAIRS)]
    val = {}

    def units(chunks):
        return [(n, p) for n in chunks for p in range(PAIRS)]

    def split_heads_rows(a):
        zero = jnp.zeros_like(a)
        return jnp.concatenate([jnp.where(first_head, a, zero),
                                jnp.where(first_head, zero, a)], axis=0)

    def group(n, c):
        return ph_ref[n * CHUNK:(n + 1) * CHUNK, c * GROUP:(c + 1) * GROUP]

    def log_decay(chunks):
        for n in chunks:
            f = lb + (1.0 - lb) * jax.nn.sigmoid(group(n, 1))
            logf = jnp.log(f)
            hi = logf.astype(BF16)
            lo = (logf - hi.astype(F32)).astype(BF16)
            val["k", n] = 1.0 - f
            val["b", n] = (jnp.dot(cum, hi, preferred_element_type=F32)
                           + jnp.dot(cum, lo, preferred_element_type=F32))

    def scaled_operands(chunks):
        for n in chunks:
            b, k = val.pop(("b", n)), val.pop(("k", n))
            b_end = b[CHUNK - 1:CHUNK, :]
            val["qe", n] = (group(n, 0) * jnp.exp(b)).astype(BF16)
            val["ke", n] = (k * jnp.exp(-b)).astype(BF16)
            val["ke_end", n] = (k * jnp.exp(b_end - b)).astype(BF16)
            val["v", n] = group(n, 2).astype(BF16)
            val["decay", n] = jnp.exp(b_end)

    def intra_scores(chunks):
        for n, p in units(chunks):
            qe, ke = val["qe", n][:, pair_cols[p]], val["ke", n][:, pair_cols[p]]
            scores = lax.dot_general(qe, split_heads_rows(ke), _NT, preferred_element_type=F32)
            val["scores", n, p] = jnp.where(causal, scores, 0.0).astype(BF16)

    def intra_out_and_update(chunks):
        for n, p in units(chunks):
            v, ke_end = val["v", n][:, pair_cols[p]], val["ke_end", n][:, pair_cols[p]]
            val["o", n, p] = jnp.dot(val.pop(("scores", n, p)), split_heads_rows(v),
                                     preferred_element_type=F32)
            kv_t = lax.dot_general(v, ke_end, _TN, preferred_element_type=F32)
            val["kv", n, p] = jnp.where(same_head, kv_t, 0.0)

    def carry_state(chunks):
        for n, p in units(chunks):
            st = st_ref[p]
            qe = val["qe", n][:, pair_cols[p]]
            val["o", n, p] = val["o", n, p] + lax.dot_general(
                qe, st.astype(BF16), _NT, preferred_element_type=F32)
            st_ref[p] = st * val["decay", n][:, pair_cols[p]] + val.pop(("kv", n, p))

    def write_out(chunks):
        for n in chunks:
            gate = group(n, 3)
            out_gain = ng_ref[...] * (gate * jax.nn.sigmoid(gate))
            out_rows = slice(out_row0 + n * CHUNK, out_row0 + (n + 1) * CHUNK)
            for p in range(PAIRS):
                o = val.pop(("o", n, p))
                o_ref[out_rows, pair_cols[p]] = (
                    _pair_rms(o, first_head) * out_gain[:, pair_cols[p]]).astype(o_ref.dtype)

    stages = [log_decay, scaled_operands, intra_scores, intra_out_and_update, carry_state, write_out]
    per_part = n_chunks // HGRN_STAGE_PARTS
    parts = [range(k * per_part, (k + 1) * per_part) for k in range(HGRN_STAGE_PARTS)]
    return [functools.partial(stage, part) for stage in stages for part in parts]


def _proj_hgrn_kernel(layer, tiles_per_seq, x_ref, g_ref, w_ref, lbp_ref, ng_ref,
                      pa_ref, o_ref, pha_ref, phb_ref, st_ref):
    i = pl.program_id(0)
    tm = pha_ref.shape[0]
    lane = lax.broadcasted_iota(jnp.int32, (1, LANES), 1)
    first_head = lane < HEAD_DIM

    lbp = lbp_ref[...]
    ex = jnp.exp(lbp - jnp.max(lbp, axis=0, keepdims=True))
    sm = ex / jnp.sum(ex, axis=0, keepdims=True)
    lb = jnp.sum(sm[:layer + 1], axis=0, keepdims=True)

    r64 = lax.broadcasted_iota(jnp.int32, (CHUNK, CHUNK), 0)
    c64 = lax.broadcasted_iota(jnp.int32, (CHUNK, CHUNK), 1)
    cum = jnp.where(c64 <= r64, 1.0, 0.0).astype(BF16)
    row = lax.broadcasted_iota(jnp.int32, (CHUNK, LANES), 0)
    col = lax.broadcasted_iota(jnp.int32, (CHUNK, LANES), 1)
    causal = (col & (HEAD_DIM - 1)) <= row
    r128 = lax.broadcasted_iota(jnp.int32, (LANES, LANES), 0)
    c128 = lax.broadcasted_iota(jnp.int32, (LANES, LANES), 1)
    same_head = (r128 < HEAD_DIM) == (c128 < HEAD_DIM)
    consts = (lb, cum, causal, same_head, first_head)

    @pl.when(i == 0)
    def _first_step():
        phb_ref[...] = jnp.zeros_like(phb_ref)
        st_ref[...] = jnp.zeros_like(st_ref)

    _interleave(_hgrn_steps(consts, phb_ref, ng_ref, st_ref, o_ref, 0),
                _project_steps(x_ref, g_ref, w_ref, slice(0, tm), pha_ref, pa_ref))
    keep = jnp.where(lax.rem(2 * i, tiles_per_seq) == 0, 0.0, 1.0)
    st_ref[...] = st_ref[...] * keep
    _interleave(_hgrn_steps(consts, pha_ref, ng_ref, st_ref, o_ref, tm),
                _project_steps(x_ref, g_ref, w_ref, slice(tm, 2 * tm), phb_ref, pa_ref))


def _proj_hgrn(x2d, gain, w_in, lower_bounds, norm_g, layer, seq, tm=ROW_TILE):
    t, d = x2d.shape
    n_h, n_a = HGRN_GROUPS * GROUP, ATTN_GROUPS * GROUP
    n_lb = lower_bounds.shape[0]
    tiles_per_seq = seq // tm
    assert seq % tm == 0 and tiles_per_seq % 2 == 0
    steps = t // (2 * tm)
    last = steps - 1
    return pl.pallas_call(
        functools.partial(_proj_hgrn_kernel, layer, tiles_per_seq),
        grid=(steps + 1,),
        in_specs=[pl.BlockSpec((2 * tm, d), lambda i: (jnp.minimum(i, last), 0)),
                  _resident((1, d)),
                  _resident((d, n_h + n_a)),
                  _resident((n_lb, GROUP)), _resident((1, GROUP))],
        out_specs=[pl.BlockSpec((2 * tm, n_a), lambda i: (jnp.minimum(i, last), 0)),
                   pl.BlockSpec((2 * tm, GROUP), lambda i: (i, 0))],
        out_shape=[jax.ShapeDtypeStruct((t, n_a), BF16),
                   jax.ShapeDtypeStruct((t + 2 * tm, GROUP), BF16)],
        scratch_shapes=[pltpu.VMEM((tm, n_h), F32), pltpu.VMEM((tm, n_h), F32),
                        pltpu.VMEM((PAIRS, LANES, LANES), F32)],
        compiler_params=pltpu.CompilerParams(
            dimension_semantics=("arbitrary",), vmem_limit_bytes=VMEM_LIMIT),
        name="proj_hgrn2",
    )(x2d, gain, w_in, lower_bounds, norm_g)


def _attn_kernel(q_ref, k_ref, v_ref, tri_ref, ng_ref, o_ref, kk_ref, vv_ref, run_ref, acc_ref):
    qi = pl.program_id(1)
    n_blocks = k_ref.shape[1] // KEY_BLOCK
    diag_blocks = Q_BLOCK // KEY_BLOCK
    lane = lax.broadcasted_iota(jnp.int32, (1, LANES), 1)
    first_head = lane < HEAD_DIM

    @pl.when(qi == 0)
    def _split_heads():
        dim_first_head = lax.broadcasted_iota(jnp.int32, (LANES, KEY_BLOCK), 0) < HEAD_DIM

        def body(j, carry):
            rows = pl.ds(pl.multiple_of(j * KEY_BLOCK, KEY_BLOCK), KEY_BLOCK)
            for p in range(PAIRS):
                kt = k_ref[0, rows, p * LANES:(p + 1) * LANES].astype(F32).T
                vb = v_ref[0, rows, p * LANES:(p + 1) * LANES]
                zero = jnp.zeros_like(vb)
                kk_ref[p, j, :, 0:KEY_BLOCK] = jnp.where(dim_first_head, kt, 0.0).astype(BF16)
                kk_ref[p, j, :, KEY_BLOCK:2 * KEY_BLOCK] = jnp.where(dim_first_head, 0.0, kt).astype(BF16)
                vv_ref[p, j, 0:KEY_BLOCK, :] = jnp.where(first_head, vb, zero)
                vv_ref[p, j, KEY_BLOCK:2 * KEY_BLOCK, :] = jnp.where(first_head, zero, vb)
            return carry
        lax.fori_loop(0, n_blocks, body, 0)

    def block(j, r0, r1, diagonal):
        rows = r1 - r0
        if diagonal:
            row = lax.broadcasted_iota(jnp.int32, (KEY_BLOCK, 2 * KEY_BLOCK), 0)
            col = lax.broadcasted_iota(jnp.int32, (KEY_BLOCK, 2 * KEY_BLOCK), 1)
            strictly_before = (col & (KEY_BLOCK - 1)) < row

        def causal(a):
            if not diagonal:
                return a
            top = jnp.where(strictly_before, a[:KEY_BLOCK], 0.0)
            return top if rows == KEY_BLOCK else jnp.concatenate([top, a[KEY_BLOCK:]], axis=0)

        def earlier(ref, p):
            if not diagonal:
                return ref[p, r0:r1]
            top = jnp.zeros((KEY_BLOCK, ref.shape[2]), F32)
            return top if rows == KEY_BLOCK else jnp.concatenate([top, ref[p, r0 + KEY_BLOCK:r1]], axis=0)

        zs, cs = [], []
        for p in range(PAIRS):
            q = q_ref[0, r0:r1, p * LANES:(p + 1) * LANES]
            z = jnp.dot(q, kk_ref[p, j], preferred_element_type=F32)
            zb = z.astype(BF16)
            pos = jnp.maximum(zb, 0.0)
            neg = zb - pos
            c = causal(pos + jnp.log(1.0 + jnp.exp(neg - pos)))
            zs.append(z)
            cs.append(c)
        incl_all = jnp.dot(jnp.concatenate(cs, axis=0), tri_ref[...], preferred_element_type=F32)

        for p in range(PAIRS):
            incl = incl_all[p * rows:(p + 1) * rows]
            run = earlier(run_ref, p)
            pr = causal(jnp.exp(zs[p] - incl - run))
            pv = jnp.dot(pr.astype(BF16), vv_ref[p, j], preferred_element_type=F32)
            tot = jnp.concatenate(
                [jnp.broadcast_to(incl[:, 0:1], (rows, KEY_BLOCK)),
                 jnp.broadcast_to(incl[:, KEY_BLOCK:KEY_BLOCK + 1], (rows, KEY_BLOCK))], axis=1)
            run_ref[p, r0:r1] = run + tot
            acc_ref[p, r0:r1] = earlier(acc_ref, p) + pv

    def tiles_needed(lowest=0):
        need = jnp.int32(lowest)
        for t in range(lowest, diag_blocks):
            live = functools.reduce(jnp.minimum, [
                jnp.min(run_ref[p, t * KEY_BLOCK:(t + 1) * KEY_BLOCK]) for p in range(PAIRS)])
            need = jnp.where(live < RUN_CUTOFF, t + 1, need)
        return need

    def limited_block(j, r0, need, least, diagonal):
        branches = [functools.partial(block, r0=r0, r1=n * KEY_BLOCK, diagonal=diagonal)
                    for n in range(least, diag_blocks + 1)]
        lax.switch(need - least, branches, j)

    first_key_block = qi * diag_blocks
    for diag in reversed(range(diag_blocks)):
        r0 = diag * KEY_BLOCK
        least = diag + DIAG_ALWAYS_TILES
        if least >= diag_blocks:
            block(first_key_block + diag, r0, Q_BLOCK, True)
        else:
            limited_block(first_key_block + diag, r0, tiles_needed(least), least, True)

    def more_blocks(state):
        t, need = state
        return jnp.logical_and(t < first_key_block, need > 0)

    def off_diagonal(state):
        t, need = state
        limited_block(first_key_block - 1 - t, 0, need, 1, False)
        return t + 1, tiles_needed()

    lax.while_loop(more_blocks, off_diagonal, (jnp.int32(0), tiles_needed()))
    for p in range(PAIRS):
        cols = slice(p * LANES, (p + 1) * LANES)
        o_ref[0, :, cols] = (_pair_rms(acc_ref[p], first_head) * ng_ref[:, cols]).astype(o_ref.dtype)


def _attention(proj_a, tri, norm_g):
    b, s, _ = proj_a.shape
    n_blocks = s // KEY_BLOCK
    return pl.pallas_call(
        _attn_kernel,
        grid=(b, s // Q_BLOCK),
        in_specs=[pl.BlockSpec((1, Q_BLOCK, GROUP), lambda bi, i: (bi, i, 0)),
                  pl.BlockSpec((1, s, GROUP), lambda bi, i: (bi, 0, 1)),
                  pl.BlockSpec((1, s, GROUP), lambda bi, i: (bi, 0, 2)),
                  _resident((2 * KEY_BLOCK, 2 * KEY_BLOCK)),
                  _resident((1, GROUP))],
        out_specs=pl.BlockSpec((1, Q_BLOCK, GROUP), lambda bi, i: (bi, i, 0)),
        out_shape=jax.ShapeDtypeStruct((b, s, GROUP), BF16),
        scratch_shapes=[pltpu.VMEM((PAIRS, n_blocks, LANES, 2 * KEY_BLOCK), BF16),
                        pltpu.VMEM((PAIRS, n_blocks, 2 * KEY_BLOCK, LANES), BF16),
                        pltpu.VMEM((PAIRS, Q_BLOCK, 2 * KEY_BLOCK), F32),
                        pltpu.VMEM((PAIRS, Q_BLOCK, LANES), F32)],
        compiler_params=pltpu.CompilerParams(
            dimension_semantics=("arbitrary", "arbitrary"), vmem_limit_bytes=VMEM_LIMIT),
        name="stickbreak_attn",
    )(proj_a, proj_a, proj_a, tri, norm_g)


def _suffix_sum_matrix():
    r = jnp.arange(2 * KEY_BLOCK)[:, None]
    c = jnp.arange(2 * KEY_BLOCK)[None, :]
    return ((r >= c) & ((r < KEY_BLOCK) == (c < KEY_BLOCK))).astype(BF16)


def _out_ffn_kernel(final, x_ref, ma_ref, mb_ref, wo_ref, g2_ref, wg_ref, wu_ref, wd_ref,
                    g3_ref, o_ref, ff_ref):
    x1 = (x_ref[...]
          + jnp.dot(ma_ref[...], wo_ref[0:GROUP, :], preferred_element_type=F32)
          + jnp.dot(mb_ref[...], wo_ref[GROUP:2 * GROUP, :], preferred_element_type=F32))
    h = _rms(x1, g2_ref[...]).astype(BF16)
    d_ff = wg_ref.shape[1]
    for c in range(d_ff // FF_CHUNK):
        cols = slice(c * FF_CHUNK, (c + 1) * FF_CHUNK)
        gate = jnp.dot(h, wg_ref[:, cols], preferred_element_type=F32)
        up = jnp.dot(h, wu_ref[:, cols], preferred_element_type=F32)
        ff_ref[:, cols] = (gate * jax.nn.sigmoid(gate) * up).astype(BF16)
    x2 = x1 + jnp.dot(ff_ref[...], wd_ref[...], preferred_element_type=F32)
    o_ref[...] = _rms(x2, g3_ref[...]) if final else x2


def _out_ffn(x2d, mix_a, mix_b, w_out, g2, w_gate, w_up, w_down, g3, final, tm=ROW_TILE):
    t, d = x2d.shape
    d_ff = w_gate.shape[1]
    row = lambda n: pl.BlockSpec((tm, n), lambda i: (i, 0))
    return pl.pallas_call(
        functools.partial(_out_ffn_kernel, final),
        grid=(t // tm,),
        in_specs=[row(d), pl.BlockSpec((tm, GROUP), lambda i: (i + 1, 0)), row(GROUP),
                  _resident((2 * GROUP, d)), _resident((1, d)),
                  _resident((d, d_ff)), _resident((d, d_ff)), _resident((d_ff, d)),
                  _resident((1, d))],
        out_specs=row(d),
        out_shape=jax.ShapeDtypeStruct((t, d), F32),
        scratch_shapes=[pltpu.VMEM((tm, d_ff), BF16)],
        compiler_params=pltpu.CompilerParams(
            dimension_semantics=("arbitrary",), vmem_limit_bytes=VMEM_LIMIT),
        name="out_ffn",
    )(x2d, mix_a, mix_b, w_out, g2, w_gate, w_up, w_down, g3)


def kernel(x, mix_norm_g, w_in, lower_bounds, hgrn_norm_g, sb_norm_g, w_out, ffn_norm_g,
           w_gate, w_up, w_down, final_norm_g):
    b, s, d = x.shape
    depth = w_in.shape[0]
    assert w_in.shape[2] == (HGRN_GROUPS + ATTN_GROUPS) * GROUP
    assert w_gate.shape[2] % FF_CHUNK == 0 and s % KEY_BLOCK == 0
    tri = _suffix_sum_matrix()
    x2d = x.reshape(b * s, d)
    for l in range(depth):
        proj_a, mix_a = _proj_hgrn(x2d, mix_norm_g[l][None], w_in[l].astype(BF16), lower_bounds,
                                   hgrn_norm_g[l][None], l, s)
        mix_b = _attention(proj_a.reshape(b, s, -1), tri, sb_norm_g[l][None])
        last = l == depth - 1
        g3 = final_norm_g[None] if last else ffn_norm_g[l][None]
        x2d = _out_ffn(x2d, mix_a, mix_b.reshape(b * s, -1),
                       w_out[l].astype(BF16), ffn_norm_g[l][None], w_gate[l].astype(BF16),
                       w_up[l].astype(BF16), w_down[l].astype(BF16), g3, last)
    if depth == 0:
        raise ValueError("depth must be positive")
    return x2d.reshape(b, s, d)
```

```python
import functools

import jax
import jax.numpy as jnp
from jax import lax
from jax.experimental import pallas as pl
from jax.experimental.pallas import tpu as pltpu

F32 = jnp.float32
BF16 = jnp.bfloat16

LANES = 128
BF16_SUBLANES = 16
FF_CHUNK = 256
PROJ_COLS = 256
HGRN_STAGE_PARTS = 2
HEAD_DIM = 64
HEADS_PER_GROUP = 8
GROUP = HEADS_PER_GROUP * HEAD_DIM
PAIRS = GROUP // LANES
HGRN_GROUPS = 4
ATTN_GROUPS = 3
CHUNK = 64
ROW_TILE = 512
KEY_BLOCK = 128
Q_BLOCK = 512
DIAG_ALWAYS_TILES = 3
RUN_CUTOFF = 95.0
EPS = 1e-6
VMEM_LIMIT = 56 * 1024 * 1024

_NT = (((1,), (1,)), ((), ()))
_TN = (((0,), (0,)), ((), ()))


def _resident(shape):
    return pl.BlockSpec(shape, lambda *_: (0,) * len(shape), pipeline_mode=pl.Buffered(1))


def _rms(x, gain):
    return x * lax.rsqrt(jnp.mean(x * x, axis=-1, keepdims=True) + EPS) * gain


def _pair_rms(o, first_head):
    sq = o * o
    s0 = jnp.sum(jnp.where(first_head, sq, 0.0), axis=-1, keepdims=True)
    s1 = jnp.sum(jnp.where(first_head, 0.0, sq), axis=-1, keepdims=True)
    ms = jnp.where(first_head, s0, s1) * (1.0 / HEAD_DIM)
    return o * lax.rsqrt(ms + EPS)


def _project_steps(x_ref, g_ref, w_ref, rows, ph_ref, pa_ref):
    h = []
    n_h = HGRN_GROUPS * GROUP

    def piece(c0):
        if not h:
            h.append(_rms(x_ref[rows, :], g_ref[...]).astype(BF16))
        r = jnp.dot(h[0], w_ref[:, c0:c0 + PROJ_COLS], preferred_element_type=F32)
        if c0 < n_h:
            ph_ref[:, c0:c0 + PROJ_COLS] = r
        else:
            if c0 - n_h < GROUP:
                r = r * (1.0 / HEAD_DIM ** 0.5)
            pa_ref[rows, c0 - n_h:c0 - n_h + PROJ_COLS] = r.astype(BF16)

    return [functools.partial(piece, c0) for c0 in range(0, w_ref.shape[1], PROJ_COLS)]


def _interleave(first, second):
    for k in range(max(len(first), len(second))):
        if k < len(first):
            first[k]()
        if k < len(second):
            second[k]()


def _hgrn_steps(consts, ph_ref, ng_ref, st_ref, o_ref, out_row0):
    lb, cum, causal, same_head, first_head = consts
    n_chunks = ph_ref.shape[0] // CHUNK
    pair_cols = [slice(p * LANES, (p + 1) * LANES) for p in range(PAIRS)]
    val = {}

    def units(chunks):
        return [(n, p) for n in chunks for p in range(PAIRS)]

    def split_heads_rows(a):
        zero = jnp.zeros_like(a)
        return jnp.concatenate([jnp.where(first_head, a, zero),
                                jnp.where(first_head, zero, a)], axis=0)

    def group(n, c):
        return ph_ref[n * CHUNK:(n + 1) * CHUNK, c * GROUP:(c + 1) * GROUP]

    def log_decay(chunks):
        for n in chunks:
            f = lb + (1.0 - lb) * jax.nn.sigmoid(group(n, 1))
            logf = jnp.log(f)
            hi = logf.astype(BF16)
            lo = (logf - hi.astype(F32)).astype(BF16)
            val["k", n] = 1.0 - f
            val["b", n] = (jnp.dot(cum, hi, preferred_element_type=F32)
                           + jnp.dot(cum, lo, preferred_element_type=F32))

    def scaled_operands(chunks):
        for n in chunks:
            b, k = val.pop(("b", n)), val.pop(("k", n))
            b_end = b[CHUNK - 1:CHUNK, :]
            val["qe", n] = (group(n, 0) * jnp.exp(b)).astype(BF16)
            val["ke", n] = (k * jnp.exp(-b)).astype(BF16)
            val["ke_end", n] = (k * jnp.exp(b_end - b)).astype(BF16)
            val["v", n] = group(n, 2).astype(BF16)
            val["decay", n] = jnp.exp(b_end)

    def intra_scores(chunks):
        for n, p in units(chunks):
            qe, ke = val["qe", n][:, pair_cols[p]], val["ke", n][:, pair_cols[p]]
            scores = lax.dot_general(qe, split_heads_rows(ke), _NT, preferred_element_type=F32)
            val["scores", n, p] = jnp.where(causal, scores, 0.0).astype(BF16)

    def intra_out_and_update(chunks):
        for n, p in units(chunks):
            v, ke_end = val["v", n][:, pair_cols[p]], val["ke_end", n][:, pair_cols[p]]
            val["o", n, p] = jnp.dot(val.pop(("scores", n, p)), split_heads_rows(v),
                                     preferred_element_type=F32)
            kv_t = lax.dot_general(v, ke_end, _TN, preferred_element_type=F32)
            val["kv", n, p] = jnp.where(same_head, kv_t, 0.0)

    def carry_state(chunks):
        for n, p in units(chunks):
            st = st_ref[p]
            qe = val["qe", n][:, pair_cols[p]]
            val["o", n, p] = val["o", n, p] + lax.dot_general(
                qe, st.astype(BF16), _NT, preferred_element_type=F32)
            st_ref[p] = st * val["decay", n][:, pair_cols[p]] + val.pop(("kv", n, p))

    def write_out(chunks):
        for n in chunks:
            gate = group(n, 3)
            out_gain = ng_ref[...] * (gate * jax.nn.sigmoid(gate))
            out_rows = slice(out_row0 + n * CHUNK, out_row0 + (n + 1) * CHUNK)
            for p in range(PAIRS):
                o = val.pop(("o", n, p))
                o_ref[out_rows, pair_cols[p]] = (
                    _pair_rms(o, first_head) * out_gain[:, pair_cols[p]]).astype(o_ref.dtype)

    stages = [log_decay, scaled_operands, intra_scores, intra_out_and_update, carry_state, write_out]
    per_part = n_chunks // HGRN_STAGE_PARTS
    parts = [range(k * per_part, (k + 1) * per_part) for k in range(HGRN_STAGE_PARTS)]
    return [functools.partial(stage, part) for stage in stages for part in parts]


def _proj_hgrn_kernel(layer, tiles_per_seq, n_cast, x_ref, g_ref, w_ref, lbp_ref, ng_ref, *refs):
    cast_in, refs = refs[:n_cast], refs[n_cast:]
    pa_ref, o_ref = refs[:2]
    cast_out, (pha_ref, phb_ref, st_ref) = refs[2:2 + n_cast], refs[2 + n_cast:]
    for src, dst in zip(cast_in, cast_out):
        dst[...] = src[...].astype(dst.dtype)
    i = pl.program_id(0)
    tm = pha_ref.shape[0]
    lane = lax.broadcasted_iota(jnp.int32, (1, LANES), 1)
    first_head = lane < HEAD_DIM

    lbp = lbp_ref[...]
    ex = jnp.exp(lbp - jnp.max(lbp, axis=0, keepdims=True))
    sm = ex / jnp.sum(ex, axis=0, keepdims=True)
    lb = jnp.sum(sm[:layer + 1], axis=0, keepdims=True)

    r64 = lax.broadcasted_iota(jnp.int32, (CHUNK, CHUNK), 0)
    c64 = lax.broadcasted_iota(jnp.int32, (CHUNK, CHUNK), 1)
    cum = jnp.where(c64 <= r64, 1.0, 0.0).astype(BF16)
    row = lax.broadcasted_iota(jnp.int32, (CHUNK, LANES), 0)
    col = lax.broadcasted_iota(jnp.int32, (CHUNK, LANES), 1)
    causal = (col & (HEAD_DIM - 1)) <= row
    r128 = lax.broadcasted_iota(jnp.int32, (LANES, LANES), 0)
    c128 = lax.broadcasted_iota(jnp.int32, (LANES, LANES), 1)
    same_head = (r128 < HEAD_DIM) == (c128 < HEAD_DIM)
    consts = (lb, cum, causal, same_head, first_head)

    @pl.when(i == 0)
    def _first_step():
        phb_ref[...] = jnp.zeros_like(phb_ref)
        st_ref[...] = jnp.zeros_like(st_ref)

    _interleave(_hgrn_steps(consts, phb_ref, ng_ref, st_ref, o_ref, 0),
                _project_steps(x_ref, g_ref, w_ref, slice(0, tm), pha_ref, pa_ref))
    keep = jnp.where(lax.rem(2 * i, tiles_per_seq) == 0, 0.0, 1.0)
    st_ref[...] = st_ref[...] * keep
    _interleave(_hgrn_steps(consts, pha_ref, ng_ref, st_ref, o_ref, tm),
                _project_steps(x_ref, g_ref, w_ref, slice(tm, 2 * tm), phb_ref, pa_ref))


def _slab_spec(shape, steps):
    span = 1
    while shape[0] % (steps // span) or (shape[0] // (steps // span)) % BF16_SUBLANES:
        span *= 2
        assert span <= steps, shape
    rows = shape[0] // (steps // span)
    return pl.BlockSpec((rows, shape[1]), lambda i: (jnp.minimum(i, steps - 1) // span, 0))


def _proj_hgrn(x2d, gain, w_in, lower_bounds, norm_g, layer, seq, later_weights, tm=ROW_TILE):
    t, d = x2d.shape
    n_h, n_a = HGRN_GROUPS * GROUP, ATTN_GROUPS * GROUP
    n_lb = lower_bounds.shape[0]
    tiles_per_seq = seq // tm
    assert seq % tm == 0 and tiles_per_seq % 2 == 0
    steps = t // (2 * tm)
    last = steps - 1
    slabs = [_slab_spec(w.shape, steps) for w in later_weights]
    out = pl.pallas_call(
        functools.partial(_proj_hgrn_kernel, layer, tiles_per_seq, len(later_weights)),
        grid=(steps + 1,),
        in_specs=[pl.BlockSpec((2 * tm, d), lambda i: (jnp.minimum(i, last), 0)),
                  _resident((1, d)),
                  _resident((d, n_h + n_a)),
                  _resident((n_lb, GROUP)), _resident((1, GROUP))] + slabs,
        out_specs=[pl.BlockSpec((2 * tm, n_a), lambda i: (jnp.minimum(i, last), 0)),
                   pl.BlockSpec((2 * tm, GROUP), lambda i: (i, 0))] + slabs,
        out_shape=[jax.ShapeDtypeStruct((t, n_a), BF16),
                   jax.ShapeDtypeStruct((t + 2 * tm, GROUP), BF16)]
                  + [jax.ShapeDtypeStruct(w.shape, BF16) for w in later_weights],
        scratch_shapes=[pltpu.VMEM((tm, n_h), F32), pltpu.VMEM((tm, n_h), F32),
                        pltpu.VMEM((PAIRS, LANES, LANES), F32)],
        compiler_params=pltpu.CompilerParams(
            dimension_semantics=("arbitrary",), vmem_limit_bytes=VMEM_LIMIT),
        name="proj_hgrn2",
    )(x2d, gain, w_in, lower_bounds, norm_g, *later_weights)
    return out[0], out[1], out[2:]


def _attn_kernel(q_ref, k_ref, v_ref, tri_ref, ng_ref, o_ref, kk_ref, vv_ref, run_ref, acc_ref):
    qi = pl.program_id(1)
    n_blocks = k_ref.shape[1] // KEY_BLOCK
    diag_blocks = Q_BLOCK // KEY_BLOCK
    lane = lax.broadcasted_iota(jnp.int32, (1, LANES), 1)
    first_head = lane < HEAD_DIM

    @pl.when(qi == 0)
    def _split_heads():
        dim_first_head = lax.broadcasted_iota(jnp.int32, (LANES, KEY_BLOCK), 0) < HEAD_DIM

        def body(j, carry):
            rows = pl.ds(pl.multiple_of(j * KEY_BLOCK, KEY_BLOCK), KEY_BLOCK)
            for p in range(PAIRS):
                kt = k_ref[0, rows, p * LANES:(p + 1) * LANES].astype(F32).T
                vb = v_ref[0, rows, p * LANES:(p + 1) * LANES]
                zero = jnp.zeros_like(vb)
                kk_ref[p, j, :, 0:KEY_BLOCK] = jnp.where(dim_first_head, kt, 0.0).astype(BF16)
                kk_ref[p, j, :, KEY_BLOCK:2 * KEY_BLOCK] = jnp.where(dim_first_head, 0.0, kt).astype(BF16)
                vv_ref[p, j, 0:KEY_BLOCK, :] = jnp.where(first_head, vb, zero)
                vv_ref[p, j, KEY_BLOCK:2 * KEY_BLOCK, :] = jnp.where(first_head, zero, vb)
            return carry
        lax.fori_loop(0, n_blocks, body, 0)

    mask_row = lax.broadcasted_iota(jnp.int32, (KEY_BLOCK, 2 * KEY_BLOCK), 0)
    mask_col = lax.broadcasted_iota(jnp.int32, (KEY_BLOCK, 2 * KEY_BLOCK), 1)
    strictly_before = (mask_col & (KEY_BLOCK - 1)) < mask_row

    def causal(a, diagonal):
        if not diagonal:
            return a
        top = jnp.where(strictly_before, a[:KEY_BLOCK], 0.0)
        return top if a.shape[0] == KEY_BLOCK else jnp.concatenate([top, a[KEY_BLOCK:]], axis=0)

    def scores(j, r0, r1, diagonal):
        zs, cs = [], []
        for p in range(PAIRS):
            q = q_ref[0, r0:r1, p * LANES:(p + 1) * LANES]
            z = jnp.dot(q, kk_ref[p, j], preferred_element_type=F32)
            zb = z.astype(BF16)
            pos = jnp.maximum(zb, 0.0)
            neg = zb - pos
            c = causal(pos + jnp.log(1.0 + jnp.exp(neg - pos)), diagonal)
            zs.append(z)
            cs.append(c)
        incl_all = jnp.dot(jnp.concatenate(cs, axis=0), tri_ref[...], preferred_element_type=F32)
        return zs, incl_all

    def accumulate(j, r0, r1, diagonal, zs, incl_all):
        rows = r1 - r0

        def earlier(ref, p):
            if not diagonal:
                return ref[p, r0:r1]
            top = jnp.zeros((KEY_BLOCK, ref.shape[2]), F32)
            return top if rows == KEY_BLOCK else jnp.concatenate([top, ref[p, r0 + KEY_BLOCK:r1]], axis=0)

        for p in range(PAIRS):
            incl = incl_all[p * rows:(p + 1) * rows]
            run = earlier(run_ref, p)
            pr = causal(jnp.exp(zs[p] - incl - run), diagonal)
            pv = jnp.dot(pr.astype(BF16), vv_ref[p, j], preferred_element_type=F32)
            tot = jnp.concatenate(
                [jnp.broadcast_to(incl[:, 0:1], (rows, KEY_BLOCK)),
                 jnp.broadcast_to(incl[:, KEY_BLOCK:KEY_BLOCK + 1], (rows, KEY_BLOCK))], axis=1)
            run_ref[p, r0:r1] = run + tot
            acc_ref[p, r0:r1] = earlier(acc_ref, p) + pv

    def block(j, r0, r1, diagonal):
        accumulate(j, r0, r1, diagonal, *scores(j, r0, r1, diagonal))

    def tiles_needed(lowest=0):
        need = jnp.int32(lowest)
        for t in range(lowest, diag_blocks):
            live = jnp.min(functools.reduce(jnp.minimum, [
                run_ref[p, t * KEY_BLOCK:(t + 1) * KEY_BLOCK] for p in range(PAIRS)]))
            need = jnp.where(live < RUN_CUTOFF, t + 1, need)
        return need

    def limited_block(j, r0, need, least, diagonal):
        branches = [functools.partial(block, r0=r0, r1=n * KEY_BLOCK, diagonal=diagonal)
                    for n in range(least, diag_blocks + 1)]
        lax.switch(need - least, branches, j)

    first_key_block = qi * diag_blocks
    always = [d for d in reversed(range(diag_blocks)) if d + DIAG_ALWAYS_TILES >= diag_blocks]
    staged = [scores(first_key_block + d, d * KEY_BLOCK, Q_BLOCK, True) for d in always]
    for d, stage in zip(always, staged):
        accumulate(first_key_block + d, d * KEY_BLOCK, Q_BLOCK, True, *stage)
    for d in reversed(range(diag_blocks)):
        least = d + DIAG_ALWAYS_TILES
        if least < diag_blocks:
            limited_block(first_key_block + d, d * KEY_BLOCK, tiles_needed(least), least, True)

    def more_blocks(state):
        t, need = state
        return jnp.logical_and(t < first_key_block, need > 0)

    def off_diagonal(state):
        t, need = state
        limited_block(first_key_block - 1 - t, 0, need, 1, False)
        return t + 1, tiles_needed()

    lax.while_loop(more_blocks, off_diagonal, (jnp.int32(0), tiles_needed()))
    for p in range(PAIRS):
        cols = slice(p * LANES, (p + 1) * LANES)
        o_ref[0, :, cols] = (_pair_rms(acc_ref[p], first_head) * ng_ref[:, cols]).astype(o_ref.dtype)


def _attention(proj_a, tri, norm_g):
    b, s, _ = proj_a.shape
    n_blocks = s // KEY_BLOCK
    return pl.pallas_call(
        _attn_kernel,
        grid=(b, s // Q_BLOCK),
        in_specs=[pl.BlockSpec((1, Q_BLOCK, GROUP), lambda bi, i: (bi, i, 0)),
                  pl.BlockSpec((1, s, GROUP), lambda bi, i: (bi, 0, 1)),
                  pl.BlockSpec((1, s, GROUP), lambda bi, i: (bi, 0, 2)),
                  _resident((2 * KEY_BLOCK, 2 * KEY_BLOCK)),
                  _resident((1, GROUP))],
        out_specs=pl.BlockSpec((1, Q_BLOCK, GROUP), lambda bi, i: (bi, i, 0)),
        out_shape=jax.ShapeDtypeStruct((b, s, GROUP), BF16),
        scratch_shapes=[pltpu.VMEM((PAIRS, n_blocks, LANES, 2 * KEY_BLOCK), BF16),
                        pltpu.VMEM((PAIRS, n_blocks, 2 * KEY_BLOCK, LANES), BF16),
                        pltpu.VMEM((PAIRS, Q_BLOCK, 2 * KEY_BLOCK), F32),
                        pltpu.VMEM((PAIRS, Q_BLOCK, LANES), F32)],
        compiler_params=pltpu.CompilerParams(
            dimension_semantics=("arbitrary", "arbitrary"), vmem_limit_bytes=VMEM_LIMIT),
        name="stickbreak_attn",
    )(proj_a, proj_a, proj_a, tri, norm_g)


def _suffix_sum_matrix():
    r = jnp.arange(2 * KEY_BLOCK)[:, None]
    c = jnp.arange(2 * KEY_BLOCK)[None, :]
    return ((r >= c) & ((r < KEY_BLOCK) == (c < KEY_BLOCK))).astype(BF16)


def _out_ffn_kernel(final, x_ref, ma_ref, mb_ref, wo_ref, g2_ref, wg_ref, wu_ref, wd_ref,
                    g3_ref, o_ref, ff_ref):
    x1 = (x_ref[...]
          + jnp.dot(ma_ref[...], wo_ref[0:GROUP, :], preferred_element_type=F32)
          + jnp.dot(mb_ref[...], wo_ref[GROUP:2 * GROUP, :], preferred_element_type=F32))
    h = _rms(x1, g2_ref[...]).astype(BF16)
    d_ff = wg_ref.shape[1]
    for c in range(d_ff // FF_CHUNK):
        cols = slice(c * FF_CHUNK, (c + 1) * FF_CHUNK)
        gate = jnp.dot(h, wg_ref[:, cols], preferred_element_type=F32)
        up = jnp.dot(h, wu_ref[:, cols], preferred_element_type=F32)
        ff_ref[:, cols] = (gate * jax.nn.sigmoid(gate) * up).astype(BF16)
    x2 = x1 + jnp.dot(ff_ref[...], wd_ref[...], preferred_element_type=F32)
    o_ref[...] = _rms(x2, g3_ref[...]) if final else x2


def _out_ffn(x2d, mix_a, mix_b, w_out, g2, w_gate, w_up, w_down, g3, final, tm=ROW_TILE):
    t, d = x2d.shape
    d_ff = w_gate.shape[1]
    row = lambda n: pl.BlockSpec((tm, n), lambda i: (i, 0))
    return pl.pallas_call(
        functools.partial(_out_ffn_kernel, final),
        grid=(t // tm,),
        in_specs=[row(d), pl.BlockSpec((tm, GROUP), lambda i: (i + 1, 0)), row(GROUP),
                  _resident((2 * GROUP, d)), _resident((1, d)),
                  _resident((d, d_ff)), _resident((d, d_ff)), _resident((d_ff, d)),
                  _resident((1, d))],
        out_specs=row(d),
        out_shape=jax.ShapeDtypeStruct((t, d), F32),
        scratch_shapes=[pltpu.VMEM((tm, d_ff), BF16)],
        compiler_params=pltpu.CompilerParams(
            dimension_semantics=("arbitrary",), vmem_limit_bytes=VMEM_LIMIT),
        name="out_ffn",
    )(x2d, mix_a, mix_b, w_out, g2, w_gate, w_up, w_down, g3)


def kernel(x, mix_norm_g, w_in, lower_bounds, hgrn_norm_g, sb_norm_g, w_out, ffn_norm_g,
           w_gate, w_up, w_down, final_norm_g):
    b, s, d = x.shape
    depth = w_in.shape[0]
    assert w_in.shape[2] == (HGRN_GROUPS + ATTN_GROUPS) * GROUP
    assert w_gate.shape[2] % FF_CHUNK == 0 and s % KEY_BLOCK == 0
    tri = _suffix_sum_matrix()
    x2d = x.reshape(b * s, d)
    for l in range(depth):
        proj_a, mix_a, (wo, wg, wu, wd) = _proj_hgrn(
            x2d, mix_norm_g[l][None], w_in[l].astype(BF16), lower_bounds, hgrn_norm_g[l][None], l, s,
            (w_out[l], w_gate[l], w_up[l], w_down[l]))
        mix_b = _attention(proj_a.reshape(b, s, -1), tri, sb_norm_g[l][None])
        last = l == depth - 1
        g3 = final_norm_g[None] if last else ffn_norm_g[l][None]
        x2d = _out_ffn(x2d, mix_a, mix_b.reshape(b * s, -1), wo, ffn_norm_g[l][None], wg, wu, wd,
                       g3, last)
    if depth == 0:
        raise ValueError("depth must be positive")
    return x2d.reshape(b, s, d)
```

```python
import functools

import jax
import jax.numpy as jnp
from jax import lax
from jax.experimental import pallas as pl
from jax.experimental.pallas import tpu as pltpu

F32 = jnp.float32
BF16 = jnp.bfloat16

LANES = 128
BF16_SUBLANES = 16
FF_CHUNK = 256
PROJ_COLS = 256
HGRN_STAGE_PARTS = 2
HEAD_DIM = 64
HEADS_PER_GROUP = 8
GROUP = HEADS_PER_GROUP * HEAD_DIM
PAIRS = GROUP // LANES
HGRN_GROUPS = 4
ATTN_GROUPS = 3
CHUNK = 64
ROW_TILE = 512
KEY_BLOCK = 128
Q_BLOCK = 512
ATTN_ROWS = 2
DIAG_ALWAYS_TILES = 3
RUN_CUTOFF = 95.0
EPS = 1e-6
VMEM_LIMIT = 56 * 1024 * 1024
ATTN_VMEM_LIMIT = 60 * 1024 * 1024

_NT = (((1,), (1,)), ((), ()))
_TN = (((0,), (0,)), ((), ()))


def _resident(shape):
    return pl.BlockSpec(shape, lambda *_: (0,) * len(shape), pipeline_mode=pl.Buffered(1))


def _rms(x, gain):
    return x * lax.rsqrt(jnp.mean(x * x, axis=-1, keepdims=True) + EPS) * gain


def _pair_rms(o, first_head):
    sq = o * o
    s0 = jnp.sum(jnp.where(first_head, sq, 0.0), axis=-1, keepdims=True)
    s1 = jnp.sum(jnp.where(first_head, 0.0, sq), axis=-1, keepdims=True)
    ms = jnp.where(first_head, s0, s1) * (1.0 / HEAD_DIM)
    return o * lax.rsqrt(ms + EPS)


def _project_steps(x_ref, g_ref, w_ref, rows, ph_ref, pa_ref):
    h = []
    n_h = HGRN_GROUPS * GROUP

    def piece(c0):
        if not h:
            h.append(_rms(x_ref[rows, :], g_ref[...]).astype(BF16))
        r = jnp.dot(h[0], w_ref[:, c0:c0 + PROJ_COLS], preferred_element_type=F32)
        if c0 < n_h:
            ph_ref[:, c0:c0 + PROJ_COLS] = r
        else:
            if c0 - n_h < GROUP:
                r = r * (1.0 / HEAD_DIM ** 0.5)
            pa_ref[rows, c0 - n_h:c0 - n_h + PROJ_COLS] = r.astype(BF16)

    return [functools.partial(piece, c0) for c0 in range(0, w_ref.shape[1], PROJ_COLS)]


def _interleave(first, second):
    for k in range(max(len(first), len(second))):
        if k < len(first):
            first[k]()
        if k < len(second):
            second[k]()


def _hgrn_steps(consts, ph_ref, ng_ref, st_ref, o_ref, out_row0):
    lb, cum, causal, same_head, first_head = consts
    n_chunks = ph_ref.shape[0] // CHUNK
    pair_cols = [slice(p * LANES, (p + 1) * LANES) for p in range(PAIRS)]
    val = {}

    def units(chunks):
        return [(n, p) for n in chunks for p in range(PAIRS)]

    def split_heads_rows(a):
        zero = jnp.zeros_like(a)
        return jnp.concatenate([jnp.where(first_head, a, zero),
                                jnp.where(first_head, zero, a)], axis=0)

    def group(n, c):
        return ph_ref[n * CHUNK:(n + 1) * CHUNK, c * GROUP:(c + 1) * GROUP]

    def log_decay(chunks):
        for n in chunks:
            f = lb + (1.0 - lb) * jax.nn.sigmoid(group(n, 1))
            logf = jnp.log(f)
            hi = logf.astype(BF16)
            lo = (logf - hi.astype(F32)).astype(BF16)
            val["k", n] = 1.0 - f
            val["b", n] = (jnp.dot(cum, hi, preferred_element_type=F32)
                           + jnp.dot(cum, lo, preferred_element_type=F32))

    def scaled_operands(chunks):
        for n in chunks:
            b, k = val.pop(("b", n)), val.pop(("k", n))
            b_end = b[CHUNK - 1:CHUNK, :]
            val["qe", n] = (group(n, 0) * jnp.exp(b)).astype(BF16)
            val["ke", n] = (k * jnp.exp(-b)).astype(BF16)
            val["ke_end", n] = (k * jnp.exp(b_end - b)).astype(BF16)
            val["v", n] = group(n, 2).astype(BF16)
            val["decay", n] = jnp.exp(b_end)

    def intra_scores(chunks):
        for n, p in units(chunks):
            qe, ke = val["qe", n][:, pair_cols[p]], val["ke", n][:, pair_cols[p]]
            scores = lax.dot_general(qe, split_heads_rows(ke), _NT, preferred_element_type=F32)
            val["scores", n, p] = jnp.where(causal, scores, 0.0).astype(BF16)

    def intra_out_and_update(chunks):
        for n, p in units(chunks):
            v, ke_end = val["v", n][:, pair_cols[p]], val["ke_end", n][:, pair_cols[p]]
            val["o", n, p] = jnp.dot(val.pop(("scores", n, p)), split_heads_rows(v),
                                     preferred_element_type=F32)
            kv_t = lax.dot_general(v, ke_end, _TN, preferred_element_type=F32)
            val["kv", n, p] = jnp.where(same_head, kv_t, 0.0)

    def carry_state(chunks):
        for n, p in units(chunks):
            st = st_ref[p]
            qe = val["qe", n][:, pair_cols[p]]
            val["o", n, p] = val["o", n, p] + lax.dot_general(
                qe, st.astype(BF16), _NT, preferred_element_type=F32)
            st_ref[p] = st * val["decay", n][:, pair_cols[p]] + val.pop(("kv", n, p))

    def write_out(chunks):
        for n in chunks:
            gate = group(n, 3)
            out_gain = ng_ref[...] * (gate * jax.nn.sigmoid(gate))
            out_rows = slice(out_row0 + n * CHUNK, out_row0 + (n + 1) * CHUNK)
            for p in range(PAIRS):
                o = val.pop(("o", n, p))
                o_ref[out_rows, pair_cols[p]] = (
                    _pair_rms(o, first_head) * out_gain[:, pair_cols[p]]).astype(o_ref.dtype)

    stages = [log_decay, scaled_operands, intra_scores, intra_out_and_update, carry_state, write_out]
    per_part = n_chunks // HGRN_STAGE_PARTS
    parts = [range(k * per_part, (k + 1) * per_part) for k in range(HGRN_STAGE_PARTS)]
    return [functools.partial(stage, part) for stage in stages for part in parts]


def _proj_hgrn_kernel(layer, tiles_per_seq, n_cast, x_ref, g_ref, w_ref, lbp_ref, ng_ref, *refs):
    cast_in, refs = refs[:n_cast], refs[n_cast:]
    pa_ref, o_ref = refs[:2]
    cast_out, (pha_ref, phb_ref, st_ref) = refs[2:2 + n_cast], refs[2 + n_cast:]
    for src, dst in zip(cast_in, cast_out):
        dst[...] = src[...].astype(dst.dtype)
    i = pl.program_id(0)
    tm = pha_ref.shape[0]
    lane = lax.broadcasted_iota(jnp.int32, (1, LANES), 1)
    first_head = lane < HEAD_DIM

    lbp = lbp_ref[...]
    ex = jnp.exp(lbp - jnp.max(lbp, axis=0, keepdims=True))
    sm = ex / jnp.sum(ex, axis=0, keepdims=True)
    lb = jnp.sum(sm[:layer + 1], axis=0, keepdims=True)

    r64 = lax.broadcasted_iota(jnp.int32, (CHUNK, CHUNK), 0)
    c64 = lax.broadcasted_iota(jnp.int32, (CHUNK, CHUNK), 1)
    cum = jnp.where(c64 <= r64, 1.0, 0.0).astype(BF16)
    row = lax.broadcasted_iota(jnp.int32, (CHUNK, LANES), 0)
    col = lax.broadcasted_iota(jnp.int32, (CHUNK, LANES), 1)
    causal = (col & (HEAD_DIM - 1)) <= row
    r128 = lax.broadcasted_iota(jnp.int32, (LANES, LANES), 0)
    c128 = lax.broadcasted_iota(jnp.int32, (LANES, LANES), 1)
    same_head = (r128 < HEAD_DIM) == (c128 < HEAD_DIM)
    consts = (lb, cum, causal, same_head, first_head)

    @pl.when(i == 0)
    def _first_step():
        phb_ref[...] = jnp.zeros_like(phb_ref)
        st_ref[...] = jnp.zeros_like(st_ref)

    _interleave(_project_steps(x_ref, g_ref, w_ref, slice(0, tm), pha_ref, pa_ref),
                _hgrn_steps(consts, phb_ref, ng_ref, st_ref, o_ref, 0))
    keep = jnp.where(lax.rem(2 * i, tiles_per_seq) == 0, 0.0, 1.0)
    st_ref[...] = st_ref[...] * keep
    _interleave(_project_steps(x_ref, g_ref, w_ref, slice(tm, 2 * tm), phb_ref, pa_ref),
                _hgrn_steps(consts, pha_ref, ng_ref, st_ref, o_ref, tm))


def _slab_spec(shape, steps):
    span = 1
    while shape[0] % (steps // span) or (shape[0] // (steps // span)) % BF16_SUBLANES:
        span *= 2
        assert span <= steps, shape
    rows = shape[0] // (steps // span)
    return pl.BlockSpec((rows, shape[1]), lambda i: (jnp.minimum(i, steps - 1) // span, 0))


def _proj_hgrn(x2d, gain, w_in, lower_bounds, norm_g, layer, seq, later_weights, tm=ROW_TILE):
    t, d = x2d.shape
    n_h, n_a = HGRN_GROUPS * GROUP, ATTN_GROUPS * GROUP
    n_lb = lower_bounds.shape[0]
    tiles_per_seq = seq // tm
    assert seq % tm == 0 and tiles_per_seq % 2 == 0
    steps = t // (2 * tm)
    last = steps - 1
    slabs = [_slab_spec(w.shape, steps) for w in later_weights]
    out = pl.pallas_call(
        functools.partial(_proj_hgrn_kernel, layer, tiles_per_seq, len(later_weights)),
        grid=(steps + 1,),
        in_specs=[pl.BlockSpec((2 * tm, d), lambda i: (jnp.minimum(i, last), 0)),
                  _resident((1, d)),
                  _resident((d, n_h + n_a)),
                  _resident((n_lb, GROUP)), _resident((1, GROUP))] + slabs,
        out_specs=[pl.BlockSpec((2 * tm, n_a), lambda i: (jnp.minimum(i, last), 0)),
                   pl.BlockSpec((2 * tm, GROUP), lambda i: (i, 0))] + slabs,
        out_shape=[jax.ShapeDtypeStruct((t, n_a), BF16),
                   jax.ShapeDtypeStruct((t + 2 * tm, GROUP), BF16)]
                  + [jax.ShapeDtypeStruct(w.shape, BF16) for w in later_weights],
        scratch_shapes=[pltpu.VMEM((tm, n_h), F32), pltpu.VMEM((tm, n_h), F32),
                        pltpu.VMEM((PAIRS, LANES, LANES), F32)],
        compiler_params=pltpu.CompilerParams(
            dimension_semantics=("arbitrary",), vmem_limit_bytes=VMEM_LIMIT),
        name="proj_hgrn2",
    )(x2d, gain, w_in, lower_bounds, norm_g, *later_weights)
    return out[0], out[1], out[2:]


def _attn_kernel(q_ref, k_ref, v_ref, tri_ref, ng_ref, o_ref, kk_ref, vv_ref, run_ref, acc_ref):
    qi = pl.program_id(1)
    n_blocks = k_ref.shape[1] // KEY_BLOCK
    diag_blocks = Q_BLOCK // KEY_BLOCK
    lane = lax.broadcasted_iota(jnp.int32, (1, LANES), 1)
    first_head = lane < HEAD_DIM
    streams = [(bi, slice(p * LANES, (p + 1) * LANES))
               for bi in range(q_ref.shape[0]) for p in range(PAIRS)]

    @pl.when(qi == 0)
    def _split_heads():
        dim_first_head = lax.broadcasted_iota(jnp.int32, (LANES, KEY_BLOCK), 0) < HEAD_DIM

        def body(j, carry):
            rows = pl.ds(pl.multiple_of(j * KEY_BLOCK, KEY_BLOCK), KEY_BLOCK)
            for s, (bi, cols) in enumerate(streams):
                kt = k_ref[bi, rows, cols].astype(F32).T
                vb = v_ref[bi, rows, cols]
                zero = jnp.zeros_like(vb)
                kk_ref[s, j, :, 0:KEY_BLOCK] = jnp.where(dim_first_head, kt, 0.0).astype(BF16)
                kk_ref[s, j, :, KEY_BLOCK:2 * KEY_BLOCK] = jnp.where(dim_first_head, 0.0, kt).astype(BF16)
                vv_ref[s, j, 0:KEY_BLOCK, :] = jnp.where(first_head, vb, zero)
                vv_ref[s, j, KEY_BLOCK:2 * KEY_BLOCK, :] = jnp.where(first_head, zero, vb)
            return carry
        lax.fori_loop(0, n_blocks, body, 0)

    mask_row = lax.broadcasted_iota(jnp.int32, (KEY_BLOCK, 2 * KEY_BLOCK), 0)
    mask_col = lax.broadcasted_iota(jnp.int32, (KEY_BLOCK, 2 * KEY_BLOCK), 1)
    strictly_before = (mask_col & (KEY_BLOCK - 1)) < mask_row

    def causal(a, diagonal):
        if not diagonal:
            return a
        top = jnp.where(strictly_before, a[:KEY_BLOCK], 0.0)
        return top if a.shape[0] == KEY_BLOCK else jnp.concatenate([top, a[KEY_BLOCK:]], axis=0)

    def scores(j, r0, r1, diagonal):
        zs, cs = [], []
        for s, (bi, cols) in enumerate(streams):
            q = q_ref[bi, r0:r1, cols]
            z = jnp.dot(q, kk_ref[s, j], preferred_element_type=F32)
            zb = z.astype(BF16)
            pos = jnp.maximum(zb, 0.0)
            neg = zb - pos
            c = causal(pos + jnp.log(1.0 + jnp.exp(neg - pos)), diagonal)
            zs.append(z)
            cs.append(c)
        incl_all = jnp.dot(jnp.concatenate(cs, axis=0), tri_ref[...], preferred_element_type=F32)
        return zs, incl_all

    def accumulate(j, r0, r1, diagonal, zs, incl_all):
        rows = r1 - r0

        def earlier(ref, s):
            if not diagonal:
                return ref[s, r0:r1]
            top = jnp.zeros((KEY_BLOCK, ref.shape[2]), F32)
            return top if rows == KEY_BLOCK else jnp.concatenate([top, ref[s, r0 + KEY_BLOCK:r1]], axis=0)

        for s in range(len(streams)):
            incl = incl_all[s * rows:(s + 1) * rows]
            run = earlier(run_ref, s)
            pr = causal(jnp.exp(zs[s] - incl - run), diagonal)
            pv = jnp.dot(pr.astype(BF16), vv_ref[s, j], preferred_element_type=F32)
            tot = jnp.concatenate(
                [jnp.broadcast_to(incl[:, 0:1], (rows, KEY_BLOCK)),
                 jnp.broadcast_to(incl[:, KEY_BLOCK:KEY_BLOCK + 1], (rows, KEY_BLOCK))], axis=1)
            run_ref[s, r0:r1] = run + tot
            acc_ref[s, r0:r1] = earlier(acc_ref, s) + pv

    def block(j, r0, r1, diagonal):
        accumulate(j, r0, r1, diagonal, *scores(j, r0, r1, diagonal))

    def tiles_needed(lowest=0):
        need = jnp.int32(lowest)
        for t in range(lowest, diag_blocks):
            live = jnp.min(functools.reduce(jnp.minimum, [
                run_ref[s, t * KEY_BLOCK:(t + 1) * KEY_BLOCK] for s in range(len(streams))]))
            need = jnp.where(live < RUN_CUTOFF, t + 1, need)
        return need

    def limited_block(j, r0, need, least, diagonal):
        branches = [functools.partial(block, r0=r0, r1=n * KEY_BLOCK, diagonal=diagonal)
                    for n in range(least, diag_blocks + 1)]
        lax.switch(need - least, branches, j)

    first_key_block = qi * diag_blocks
    always = [d for d in reversed(range(diag_blocks)) if d + DIAG_ALWAYS_TILES >= diag_blocks]
    staged = [scores(first_key_block + d, d * KEY_BLOCK, Q_BLOCK, True) for d in always]
    for d, stage in zip(always, staged):
        accumulate(first_key_block + d, d * KEY_BLOCK, Q_BLOCK, True, *stage)
    for d in reversed(range(diag_blocks)):
        least = d + DIAG_ALWAYS_TILES
        if least < diag_blocks:
            limited_block(first_key_block + d, d * KEY_BLOCK, tiles_needed(least), least, True)

    def more_blocks(state):
        t, need = state
        return jnp.logical_and(t < first_key_block, need > 0)

    def off_diagonal(state):
        t, need = state
        limited_block(first_key_block - 1 - t, 0, need, 1, False)
        return t + 1, tiles_needed()

    lax.while_loop(more_blocks, off_diagonal, (jnp.int32(0), tiles_needed()))
    for s, (bi, cols) in enumerate(streams):
        o_ref[bi, :, cols] = (_pair_rms(acc_ref[s], first_head) * ng_ref[:, cols]).astype(o_ref.dtype)


def _attention(proj_a, tri, norm_g):
    b, s, _ = proj_a.shape
    n_blocks = s // KEY_BLOCK
    assert b % ATTN_ROWS == 0
    n_streams = ATTN_ROWS * PAIRS
    return pl.pallas_call(
        _attn_kernel,
        grid=(b // ATTN_ROWS, s // Q_BLOCK),
        in_specs=[pl.BlockSpec((ATTN_ROWS, Q_BLOCK, GROUP), lambda bi, i: (bi, i, 0)),
                  pl.BlockSpec((ATTN_ROWS, s, GROUP), lambda bi, i: (bi, 0, 1)),
                  pl.BlockSpec((ATTN_ROWS, s, GROUP), lambda bi, i: (bi, 0, 2)),
                  _resident((2 * KEY_BLOCK, 2 * KEY_BLOCK)),
                  _resident((1, GROUP))],
        out_specs=pl.BlockSpec((ATTN_ROWS, Q_BLOCK, GROUP), lambda bi, i: (bi, i, 0)),
        out_shape=jax.ShapeDtypeStruct((b, s, GROUP), BF16),
        scratch_shapes=[pltpu.VMEM((n_streams, n_blocks, LANES, 2 * KEY_BLOCK), BF16),
                        pltpu.VMEM((n_streams, n_blocks, 2 * KEY_BLOCK, LANES), BF16),
                        pltpu.VMEM((n_streams, Q_BLOCK, 2 * KEY_BLOCK), F32),
                        pltpu.VMEM((n_streams, Q_BLOCK, LANES), F32)],
        compiler_params=pltpu.CompilerParams(
            dimension_semantics=("arbitrary", "arbitrary"), vmem_limit_bytes=ATTN_VMEM_LIMIT),
        name="stickbreak_attn",
    )(proj_a, proj_a, proj_a, tri, norm_g)


def _suffix_sum_matrix():
    r = jnp.arange(2 * KEY_BLOCK)[:, None]
    c = jnp.arange(2 * KEY_BLOCK)[None, :]
    return ((r >= c) & ((r < KEY_BLOCK) == (c < KEY_BLOCK))).astype(BF16)


def _out_ffn_kernel(final, x_ref, ma_ref, mb_ref, wo_ref, g2_ref, wg_ref, wu_ref, wd_ref,
                    g3_ref, o_ref, ff_ref):
    x1 = (x_ref[...]
          + jnp.dot(ma_ref[...], wo_ref[0:GROUP, :], preferred_element_type=F32)
          + jnp.dot(mb_ref[...], wo_ref[GROUP:2 * GROUP, :], preferred_element_type=F32))
    h = _rms(x1, g2_ref[...]).astype(BF16)
    d_ff = wg_ref.shape[1]
    for c in range(d_ff // FF_CHUNK):
        cols = slice(c * FF_CHUNK, (c + 1) * FF_CHUNK)
        gate = jnp.dot(h, wg_ref[:, cols], preferred_element_type=F32)
        up = jnp.dot(h, wu_ref[:, cols], preferred_element_type=F32)
        ff_ref[:, cols] = (gate * jax.nn.sigmoid(gate) * up).astype(BF16)
    x2 = x1 + jnp.dot(ff_ref[...], wd_ref[...], preferred_element_type=F32)
    o_ref[...] = _rms(x2, g3_ref[...]) if final else x2


def _out_ffn(x2d, mix_a, mix_b, w_out, g2, w_gate, w_up, w_down, g3, final, tm=ROW_TILE):
    t, d = x2d.shape
    d_ff = w_gate.shape[1]
    row = lambda n: pl.BlockSpec((tm, n), lambda i: (i, 0))
    return pl.pallas_call(
        functools.partial(_out_ffn_kernel, final),
        grid=(t // tm,),
        in_specs=[row(d), pl.BlockSpec((tm, GROUP), lambda i: (i + 1, 0)), row(GROUP),
                  _resident((2 * GROUP, d)), _resident((1, d)),
                  _resident((d, d_ff)), _resident((d, d_ff)), _resident((d_ff, d)),
                  _resident((1, d))],
        out_specs=row(d),
        out_shape=jax.ShapeDtypeStruct((t, d), F32),
        scratch_shapes=[pltpu.VMEM((tm, d_ff), BF16)],
        compiler_params=pltpu.CompilerParams(
            dimension_semantics=("arbitrary",), vmem_limit_bytes=VMEM_LIMIT),
        name="out_ffn",
    )(x2d, mix_a, mix_b, w_out, g2, w_gate, w_up, w_down, g3)


def kernel(x, mix_norm_g, w_in, lower_bounds, hgrn_norm_g, sb_norm_g, w_out, ffn_norm_g,
           w_gate, w_up, w_down, final_norm_g):
    b, s, d = x.shape
    depth = w_in.shape[0]
    assert w_in.shape[2] == (HGRN_GROUPS + ATTN_GROUPS) * GROUP
    assert w_gate.shape[2] % FF_CHUNK == 0 and s % KEY_BLOCK == 0
    tri = _suffix_sum_matrix()
    x2d = x.reshape(b * s, d)
    for l in range(depth):
        proj_a, mix_a, (wo, wg, wu, wd) = _proj_hgrn(
            x2d, mix_norm_g[l][None], w_in[l].astype(BF16), lower_bounds, hgrn_norm_g[l][None], l, s,
            (w_out[l], w_gate[l], w_up[l], w_down[l]))
        mix_b = _attention(proj_a.reshape(b, s, -1), tri, sb_norm_g[l][None])
        last = l == depth - 1
        g3 = final_norm_g[None] if last else ffn_norm_g[l][None]
        x2d = _out_ffn(x2d, mix_a, mix_b.reshape(b * s, -1), wo, ffn_norm_g[l][None], wg, wu, wd,
                       g3, last)
    if depth == 0:
        raise ValueError("depth must be positive")
    return x2d.reshape(b, s, d)
```

```python
import functools

import jax
import jax.numpy as jnp
from jax import lax
from jax.experimental import pallas as pl
from jax.experimental.pallas import tpu as pltpu

F32 = jnp.float32
BF16 = jnp.bfloat16

LANES = 128
BF16_SUBLANES = 16
FF_CHUNK = 256
PROJ_COLS = 256
HGRN_STAGE_PARTS = 2
HEAD_DIM = 64
HEADS_PER_GROUP = 8
GROUP = HEADS_PER_GROUP * HEAD_DIM
PAIRS = GROUP // LANES
HGRN_GROUPS = 4
ATTN_GROUPS = 3
CHUNK = 64
ROW_TILE = 512
KEY_BLOCK = 128
Q_BLOCK = 512
ATTN_ROWS = 2
DIAG_ALWAYS_TILES = 3
RUN_CUTOFF = 95.0
EPS = 1e-6
VMEM_LIMIT = 56 * 1024 * 1024
ATTN_VMEM_LIMIT = 60 * 1024 * 1024

_NT = (((1,), (1,)), ((), ()))
_TN = (((0,), (0,)), ((), ()))


def _resident(shape):
    return pl.BlockSpec(shape, lambda *_: (0,) * len(shape), pipeline_mode=pl.Buffered(1))


def _rms(x, gain):
    return x * lax.rsqrt(jnp.mean(x * x, axis=-1, keepdims=True) + EPS) * gain


def _pair_rms(o, first_head):
    sq = o * o
    s0 = jnp.sum(jnp.where(first_head, sq, 0.0), axis=-1, keepdims=True)
    s1 = jnp.sum(jnp.where(first_head, 0.0, sq), axis=-1, keepdims=True)
    ms = jnp.where(first_head, s0, s1) * (1.0 / HEAD_DIM)
    return o * lax.rsqrt(ms + EPS)


def _project_steps(x_ref, g_ref, w_ref, rows, ph_ref, pa_ref):
    h = []
    n_h = HGRN_GROUPS * GROUP

    def piece(c0):
        if not h:
            h.append(_rms(x_ref[rows, :], g_ref[...]).astype(BF16))
        r = jnp.dot(h[0], w_ref[:, c0:c0 + PROJ_COLS], preferred_element_type=F32)
        if c0 < n_h:
            ph_ref[:, c0:c0 + PROJ_COLS] = r
        else:
            if c0 - n_h < GROUP:
                r = r * (1.0 / HEAD_DIM ** 0.5)
            pa_ref[rows, c0 - n_h:c0 - n_h + PROJ_COLS] = r.astype(BF16)

    return [functools.partial(piece, c0) for c0 in range(0, w_ref.shape[1], PROJ_COLS)]


def _interleave(first, second):
    for k in range(max(len(first), len(second))):
        if k < len(first):
            first[k]()
        if k < len(second):
            second[k]()


def _hgrn_steps(consts, ph_ref, ng_ref, st_ref, o_ref, out_row0):
    lb, cum, causal, same_head, first_head = consts
    n_chunks = ph_ref.shape[0] // CHUNK
    pair_cols = [slice(p * LANES, (p + 1) * LANES) for p in range(PAIRS)]
    val = {}

    def units(chunks):
        return [(n, p) for n in chunks for p in range(PAIRS)]

    def split_heads_rows(a):
        zero = jnp.zeros_like(a)
        return jnp.concatenate([jnp.where(first_head, a, zero),
                                jnp.where(first_head, zero, a)], axis=0)

    def group(n, c):
        return ph_ref[n * CHUNK:(n + 1) * CHUNK, c * GROUP:(c + 1) * GROUP]

    def log_decay(chunks):
        for n in chunks:
            f = lb + (1.0 - lb) * jax.nn.sigmoid(group(n, 1))
            logf = jnp.log(f)
            hi = logf.astype(BF16)
            lo = (logf - hi.astype(F32)).astype(BF16)
            val["k", n] = 1.0 - f
            val["b", n] = (jnp.dot(cum, hi, preferred_element_type=F32)
                           + jnp.dot(cum, lo, preferred_element_type=F32))

    def scaled_operands(chunks):
        for n in chunks:
            b, k = val.pop(("b", n)), val.pop(("k", n))
            b_end = b[CHUNK - 1:CHUNK, :]
            val["qe", n] = (group(n, 0) * jnp.exp(b)).astype(BF16)
            val["ke", n] = (k * jnp.exp(-b)).astype(BF16)
            val["ke_end", n] = (k * jnp.exp(b_end - b)).astype(BF16)
            val["v", n] = group(n, 2).astype(BF16)
            val["decay", n] = jnp.exp(b_end)

    def intra_scores(chunks):
        for n, p in units(chunks):
            qe, ke = val["qe", n][:, pair_cols[p]], val["ke", n][:, pair_cols[p]]
            scores = lax.dot_general(qe, split_heads_rows(ke), _NT, preferred_element_type=F32)
            val["scores", n, p] = jnp.where(causal, scores, 0.0).astype(BF16)

    def intra_out_and_update(chunks):
        for n, p in units(chunks):
            v, ke_end = val["v", n][:, pair_cols[p]], val["ke_end", n][:, pair_cols[p]]
            val["o", n, p] = jnp.dot(val.pop(("scores", n, p)), split_heads_rows(v),
                                     preferred_element_type=F32)
            kv_t = lax.dot_general(v, ke_end, _TN, preferred_element_type=F32)
            val["kv", n, p] = jnp.where(same_head, kv_t, 0.0)

    def carry_state(chunks):
        for n, p in units(chunks):
            st = st_ref[p]
            qe = val["qe", n][:, pair_cols[p]]
            val["o", n, p] = val["o", n, p] + lax.dot_general(
                qe, st.astype(BF16), _NT, preferred_element_type=F32)
            st_ref[p] = st * val["decay", n][:, pair_cols[p]] + val.pop(("kv", n, p))

    def write_out(chunks):
        for n in chunks:
            gate = group(n, 3)
            out_gain = ng_ref[...] * (gate * jax.nn.sigmoid(gate))
            out_rows = slice(out_row0 + n * CHUNK, out_row0 + (n + 1) * CHUNK)
            for p in range(PAIRS):
                o = val.pop(("o", n, p))
                o_ref[out_rows, pair_cols[p]] = (
                    _pair_rms(o, first_head) * out_gain[:, pair_cols[p]]).astype(o_ref.dtype)

    stages = [log_decay, scaled_operands, intra_scores, intra_out_and_update, carry_state, write_out]
    per_part = n_chunks // HGRN_STAGE_PARTS
    parts = [range(k * per_part, (k + 1) * per_part) for k in range(HGRN_STAGE_PARTS)]
    return [functools.partial(stage, part) for stage in stages for part in parts]


def _proj_hgrn_kernel(layer, tiles_per_seq, n_cast, x_ref, g_ref, w_ref, lbp_ref, ng_ref, *refs):
    cast_in, refs = refs[:n_cast], refs[n_cast:]
    pa_ref, o_ref = refs[:2]
    cast_out, (pha_ref, phb_ref, st_ref) = refs[2:2 + n_cast], refs[2 + n_cast:]
    for src, dst in zip(cast_in, cast_out):
        dst[...] = src[...].astype(dst.dtype)
    i = pl.program_id(0)
    tm = pha_ref.shape[0]
    lane = lax.broadcasted_iota(jnp.int32, (1, LANES), 1)
    first_head = lane < HEAD_DIM

    lbp = lbp_ref[...]
    ex = jnp.exp(lbp - jnp.max(lbp, axis=0, keepdims=True))
    sm = ex / jnp.sum(ex, axis=0, keepdims=True)
    lb = jnp.sum(sm[:layer + 1], axis=0, keepdims=True)

    r64 = lax.broadcasted_iota(jnp.int32, (CHUNK, CHUNK), 0)
    c64 = lax.broadcasted_iota(jnp.int32, (CHUNK, CHUNK), 1)
    cum = jnp.where(c64 <= r64, 1.0, 0.0).astype(BF16)
    row = lax.broadcasted_iota(jnp.int32, (CHUNK, LANES), 0)
    col = lax.broadcasted_iota(jnp.int32, (CHUNK, LANES), 1)
    causal = (col & (HEAD_DIM - 1)) <= row
    r128 = lax.broadcasted_iota(jnp.int32, (LANES, LANES), 0)
    c128 = lax.broadcasted_iota(jnp.int32, (LANES, LANES), 1)
    same_head = (r128 < HEAD_DIM) == (c128 < HEAD_DIM)
    consts = (lb, cum, causal, same_head, first_head)

    @pl.when(i == 0)
    def _first_step():
        phb_ref[...] = jnp.zeros_like(phb_ref)
        st_ref[...] = jnp.zeros_like(st_ref)

    _interleave(_project_steps(x_ref, g_ref, w_ref, slice(0, tm), pha_ref, pa_ref),
                _hgrn_steps(consts, phb_ref, ng_ref, st_ref, o_ref, 0))
    keep = jnp.where(lax.rem(2 * i, tiles_per_seq) == 0, 0.0, 1.0)
    st_ref[...] = st_ref[...] * keep
    _interleave(_project_steps(x_ref, g_ref, w_ref, slice(tm, 2 * tm), phb_ref, pa_ref),
                _hgrn_steps(consts, pha_ref, ng_ref, st_ref, o_ref, tm))


def _slab_spec(shape, steps):
    span = 1
    while shape[0] % (steps // span) or (shape[0] // (steps // span)) % BF16_SUBLANES:
        span *= 2
        assert span <= steps, shape
    rows = shape[0] // (steps // span)
    return pl.BlockSpec((rows, shape[1]), lambda i: (jnp.minimum(i, steps - 1) // span, 0))


def _proj_hgrn(x2d, gain, w_in, lower_bounds, norm_g, layer, seq, later_weights, tm=ROW_TILE):
    t, d = x2d.shape
    n_h, n_a = HGRN_GROUPS * GROUP, ATTN_GROUPS * GROUP
    n_lb = lower_bounds.shape[0]
    tiles_per_seq = seq // tm
    assert seq % tm == 0 and tiles_per_seq % 2 == 0
    steps = t // (2 * tm)
    last = steps - 1
    slabs = [_slab_spec(w.shape, steps) for w in later_weights]
    out = pl.pallas_call(
        functools.partial(_proj_hgrn_kernel, layer, tiles_per_seq, len(later_weights)),
        grid=(steps + 1,),
        in_specs=[pl.BlockSpec((2 * tm, d), lambda i: (jnp.minimum(i, last), 0)),
                  _resident((1, d)),
                  _resident((d, n_h + n_a)),
                  _resident((n_lb, GROUP)), _resident((1, GROUP))] + slabs,
        out_specs=[pl.BlockSpec((2 * tm, n_a), lambda i: (jnp.minimum(i, last), 0)),
                   pl.BlockSpec((2 * tm, GROUP), lambda i: (i, 0))] + slabs,
        out_shape=[jax.ShapeDtypeStruct((t, n_a), BF16),
                   jax.ShapeDtypeStruct((t + 2 * tm, GROUP), BF16)]
                  + [jax.ShapeDtypeStruct(w.shape, BF16) for w in later_weights],
        scratch_shapes=[pltpu.VMEM((tm, n_h), F32), pltpu.VMEM((tm, n_h), F32),
                        pltpu.VMEM((PAIRS, LANES, LANES), F32)],
        compiler_params=pltpu.CompilerParams(
            dimension_semantics=("arbitrary",), vmem_limit_bytes=VMEM_LIMIT),
        name="proj_hgrn2",
    )(x2d, gain, w_in, lower_bounds, norm_g, *later_weights)
    return out[0], out[1], out[2:]


def _attn_kernel(q_ref, k_ref, v_ref, tri_ref, ng_ref, o_ref, kk_ref, vv_ref, run_ref, acc_ref):
    qi = pl.program_id(1)
    n_blocks = k_ref.shape[1] // KEY_BLOCK
    diag_blocks = Q_BLOCK // KEY_BLOCK
    lane = lax.broadcasted_iota(jnp.int32, (1, LANES), 1)
    first_head = lane < HEAD_DIM
    streams = [(bi, slice(p * LANES, (p + 1) * LANES))
               for bi in range(q_ref.shape[0]) for p in range(PAIRS)]

    @pl.when(qi == 0)
    def _split_heads():
        dim_first_head = lax.broadcasted_iota(jnp.int32, (LANES, KEY_BLOCK), 0) < HEAD_DIM

        def body(j, carry):
            rows = pl.ds(pl.multiple_of(j * KEY_BLOCK, KEY_BLOCK), KEY_BLOCK)
            for s, (bi, cols) in enumerate(streams):
                kt = k_ref[bi, rows, cols].astype(F32).T
                vb = v_ref[bi, rows, cols]
                zero = jnp.zeros_like(vb)
                kk_ref[s, j, :, 0:KEY_BLOCK] = jnp.where(dim_first_head, kt, 0.0).astype(BF16)
                kk_ref[s, j, :, KEY_BLOCK:2 * KEY_BLOCK] = jnp.where(dim_first_head, 0.0, kt).astype(BF16)
                vv_ref[s, j, 0:KEY_BLOCK, :] = jnp.where(first_head, vb, zero)
                vv_ref[s, j, KEY_BLOCK:2 * KEY_BLOCK, :] = jnp.where(first_head, zero, vb)
            return carry
        lax.fori_loop(0, n_blocks, body, 0)

    mask_row = lax.broadcasted_iota(jnp.int32, (KEY_BLOCK, 2 * KEY_BLOCK), 0)
    mask_col = lax.broadcasted_iota(jnp.int32, (KEY_BLOCK, 2 * KEY_BLOCK), 1)
    strictly_before = (mask_col & (KEY_BLOCK - 1)) < mask_row

    def causal(a, diagonal):
        if not diagonal:
            return a
        top = jnp.where(strictly_before, a[:KEY_BLOCK], 0.0)
        return top if a.shape[0] == KEY_BLOCK else jnp.concatenate([top, a[KEY_BLOCK:]], axis=0)

    def scores(j, r0, r1, diagonal):
        zs, cs = [], []
        for s, (bi, cols) in enumerate(streams):
            q = q_ref[bi, r0:r1, cols]
            z = jnp.dot(q, kk_ref[s, j], preferred_element_type=F32)
            zb = z.astype(BF16)
            pos = jnp.maximum(zb, 0.0)
            neg = zb - pos
            c = causal(pos + jnp.log(1.0 + jnp.exp(neg - pos)), diagonal)
            zs.append(z)
            cs.append(c)
        incl_all = jnp.dot(jnp.concatenate(cs, axis=0), tri_ref[...], preferred_element_type=F32)
        return zs, incl_all

    def accumulate(j, r0, r1, diagonal, zs, incl_all):
        rows = r1 - r0

        def earlier(ref, s):
            if not diagonal:
                return ref[s, r0:r1]
            top = jnp.zeros((KEY_BLOCK, ref.shape[2]), F32)
            return top if rows == KEY_BLOCK else jnp.concatenate([top, ref[s, r0 + KEY_BLOCK:r1]], axis=0)

        for s in range(len(streams)):
            incl = incl_all[s * rows:(s + 1) * rows]
            run = earlier(run_ref, s)
            pr = causal(jnp.exp(zs[s] - incl - run), diagonal)
            pv = jnp.dot(pr.astype(BF16), vv_ref[s, j], preferred_element_type=F32)
            tot = jnp.concatenate(
                [jnp.broadcast_to(incl[:, 0:1], (rows, KEY_BLOCK)),
                 jnp.broadcast_to(incl[:, KEY_BLOCK:KEY_BLOCK + 1], (rows, KEY_BLOCK))], axis=1)
            run_ref[s, r0:r1] = run + tot
            acc_ref[s, r0:r1] = earlier(acc_ref, s) + pv

    def block(j, r0, r1, diagonal):
        accumulate(j, r0, r1, diagonal, *scores(j, r0, r1, diagonal))

    def tiles_needed(lowest=0):
        need = jnp.int32(lowest)
        for t in range(lowest, diag_blocks):
            live = jnp.min(functools.reduce(jnp.minimum, [
                run_ref[s, t * KEY_BLOCK:(t + 1) * KEY_BLOCK] for s in range(len(streams))]))
            need = jnp.where(live < RUN_CUTOFF, t + 1, need)
        return need

    def limited_block(j, r0, need, least, diagonal):
        branches = [functools.partial(block, r0=r0, r1=n * KEY_BLOCK, diagonal=diagonal)
                    for n in range(least, diag_blocks + 1)]
        lax.switch(need - least, branches, j)

    first_key_block = qi * diag_blocks
    always = [d for d in reversed(range(diag_blocks)) if d + DIAG_ALWAYS_TILES >= diag_blocks]
    staged = [scores(first_key_block + d, d * KEY_BLOCK, Q_BLOCK, True) for d in always]
    for d, stage in zip(always, staged):
        accumulate(first_key_block + d, d * KEY_BLOCK, Q_BLOCK, True, *stage)
    for d in reversed(range(diag_blocks)):
        least = d + DIAG_ALWAYS_TILES
        if least < diag_blocks:
            limited_block(first_key_block + d, d * KEY_BLOCK, tiles_needed(least), least, True)

    def more_blocks(state):
        t, need = state
        return jnp.logical_and(t < first_key_block, need > 0)

    def off_diagonal(state):
        t, need = state
        limited_block(first_key_block - 1 - t, 0, need, 1, False)
        return t + 1, tiles_needed()

    lax.while_loop(more_blocks, off_diagonal, (jnp.int32(0), tiles_needed()))
    for s, (bi, cols) in enumerate(streams):
        o_ref[bi, :, cols] = (_pair_rms(acc_ref[s], first_head) * ng_ref[:, cols]).astype(o_ref.dtype)


def _attention(proj_a, tri, norm_g):
    b, s, _ = proj_a.shape
    n_blocks = s // KEY_BLOCK
    assert b % ATTN_ROWS == 0
    n_streams = ATTN_ROWS * PAIRS
    return pl.pallas_call(
        _attn_kernel,
        grid=(b // ATTN_ROWS, s // Q_BLOCK),
        in_specs=[pl.BlockSpec((ATTN_ROWS, Q_BLOCK, GROUP), lambda bi, i: (bi, i, 0)),
                  pl.BlockSpec((ATTN_ROWS, s, GROUP), lambda bi, i: (bi, 0, 1)),
                  pl.BlockSpec((ATTN_ROWS, s, GROUP), lambda bi, i: (bi, 0, 2)),
                  _resident((2 * KEY_BLOCK, 2 * KEY_BLOCK)),
                  _resident((1, GROUP))],
        out_specs=pl.BlockSpec((ATTN_ROWS, Q_BLOCK, GROUP), lambda bi, i: (bi, i, 0)),
        out_shape=jax.ShapeDtypeStruct((b, s, GROUP), BF16),
        scratch_shapes=[pltpu.VMEM((n_streams, n_blocks, LANES, 2 * KEY_BLOCK), BF16),
                        pltpu.VMEM((n_streams, n_blocks, 2 * KEY_BLOCK, LANES), BF16),
                        pltpu.VMEM((n_streams, Q_BLOCK, 2 * KEY_BLOCK), F32),
                        pltpu.VMEM((n_streams, Q_BLOCK, LANES), F32)],
        compiler_params=pltpu.CompilerParams(
            dimension_semantics=("arbitrary", "arbitrary"), vmem_limit_bytes=ATTN_VMEM_LIMIT),
        name="stickbreak_attn",
    )(proj_a, proj_a, proj_a, tri, norm_g)


def _suffix_sum_matrix():
    r = jnp.arange(2 * KEY_BLOCK)[:, None]
    c = jnp.arange(2 * KEY_BLOCK)[None, :]
    return ((r >= c) & ((r < KEY_BLOCK) == (c < KEY_BLOCK))).astype(BF16)


def _out_ffn_kernel(final, x_ref, ma0_ref, ma1_ref, mb_ref, wo_ref, g2_ref, wg_ref, wu_ref, wd_ref,
                    g3_ref, o_ref, ff_ref):
    tm = ma0_ref.shape[0]
    tiles = [(slice(0, tm), ma0_ref), (slice(tm, 2 * tm), ma1_ref)]
    for rows, ma_ref in tiles:
        o_ref[rows, :] = (x_ref[rows, :]
                          + jnp.dot(ma_ref[...], wo_ref[0:GROUP, :], preferred_element_type=F32)
                          + jnp.dot(mb_ref[rows, :], wo_ref[GROUP:2 * GROUP, :],
                                    preferred_element_type=F32))
    hs = [_rms(o_ref[rows, :], g2_ref[...]).astype(BF16) for rows, _ in tiles]
    d_ff = wg_ref.shape[1]
    for c in range(d_ff // FF_CHUNK):
        cols = slice(c * FF_CHUNK, (c + 1) * FF_CHUNK)
        for (rows, _), h in zip(tiles, hs):
            gate = jnp.dot(h, wg_ref[:, cols], preferred_element_type=F32)
            up = jnp.dot(h, wu_ref[:, cols], preferred_element_type=F32)
            ff_ref[rows, cols] = (gate * jax.nn.sigmoid(gate) * up).astype(BF16)
    for rows, _ in tiles:
        x2 = o_ref[rows, :] + jnp.dot(ff_ref[rows, :], wd_ref[...], preferred_element_type=F32)
        o_ref[rows, :] = _rms(x2, g3_ref[...]) if final else x2


def _out_ffn(x2d, mix_a, mix_b, w_out, g2, w_gate, w_up, w_down, g3, final, tm=ROW_TILE):
    t, d = x2d.shape
    d_ff = w_gate.shape[1]
    assert t % (2 * tm) == 0
    row = lambda n: pl.BlockSpec((2 * tm, n), lambda i: (i, 0))
    return pl.pallas_call(
        functools.partial(_out_ffn_kernel, final),
        grid=(t // (2 * tm),),
        in_specs=[row(d),
                  pl.BlockSpec((tm, GROUP), lambda i: (2 * i + 1, 0)),
                  pl.BlockSpec((tm, GROUP), lambda i: (2 * i + 2, 0)),
                  row(GROUP),
                  _resident((2 * GROUP, d)), _resident((1, d)),
                  _resident((d, d_ff)), _resident((d, d_ff)), _resident((d_ff, d)),
                  _resident((1, d))],
        out_specs=row(d),
        out_shape=jax.ShapeDtypeStruct((t, d), F32),
        scratch_shapes=[pltpu.VMEM((2 * tm, d_ff), BF16)],
        compiler_params=pltpu.CompilerParams(
            dimension_semantics=("arbitrary",), vmem_limit_bytes=ATTN_VMEM_LIMIT),
        name="out_ffn",
    )(x2d, mix_a, mix_a, mix_b, w_out, g2, w_gate, w_up, w_down, g3)


def kernel(x, mix_norm_g, w_in, lower_bounds, hgrn_norm_g, sb_norm_g, w_out, ffn_norm_g,
           w_gate, w_up, w_down, final_norm_g):
    b, s, d = x.shape
    depth = w_in.shape[0]
    assert w_in.shape[2] == (HGRN_GROUPS + ATTN_GROUPS) * GROUP
    assert w_gate.shape[2] % FF_CHUNK == 0 and s % KEY_BLOCK == 0
    tri = _suffix_sum_matrix()
    x2d = x.reshape(b * s, d)
    for l in range(depth):
        proj_a, mix_a, (wo, wg, wu, wd) = _proj_hgrn(
            x2d, mix_norm_g[l][None], w_in[l].astype(BF16), lower_bounds, hgrn_norm_g[l][None], l, s,
            (w_out[l], w_gate[l], w_up[l], w_down[l]))
        mix_b = _attention(proj_a.reshape(b, s, -1), tri, sb_norm_g[l][None])
        last = l == depth - 1
        g3 = final_norm_g[None] if last else ffn_norm_g[l][None]
        x2d = _out_ffn(x2d, mix_a, mix_b.reshape(b * s, -1), wo, ffn_norm_g[l][None], wg, wu, wd,
                       g3, last)
    if depth == 0:
        raise ValueError("depth must be positive")
    return x2d.reshape(b, s, d)
```

```python
import functools

import jax
import jax.numpy as jnp
from jax import lax
from jax.experimental import pallas as pl
from jax.experimental.pallas import tpu as pltpu

F32 = jnp.float32
BF16 = jnp.bfloat16

LANES = 128
BF16_SUBLANES = 16
FF_CHUNK = 256
PROJ_COLS = 256
HGRN_STAGE_PARTS = 2
HEAD_DIM = 64
HEADS_PER_GROUP = 8
GROUP = HEADS_PER_GROUP * HEAD_DIM
PAIRS = GROUP // LANES
HGRN_GROUPS = 4
ATTN_GROUPS = 3
CHUNK = 64
ROW_TILE = 512
KEY_BLOCK = 128
Q_BLOCK = 512
ATTN_ROWS = 2
DIAG_ALWAYS_TILES = 3
RUN_CUTOFF = 95.0
EPS = 1e-6
VMEM_LIMIT = 56 * 1024 * 1024
ATTN_VMEM_LIMIT = 60 * 1024 * 1024

_NT = (((1,), (1,)), ((), ()))
_TN = (((0,), (0,)), ((), ()))


def _resident(shape):
    return pl.BlockSpec(shape, lambda *_: (0,) * len(shape), pipeline_mode=pl.Buffered(1))


def _rms(x, gain):
    return x * lax.rsqrt(jnp.mean(x * x, axis=-1, keepdims=True) + EPS) * gain


def _pair_rms(o, first_head):
    sq = o * o
    s0 = jnp.sum(jnp.where(first_head, sq, 0.0), axis=-1, keepdims=True)
    s1 = jnp.sum(jnp.where(first_head, 0.0, sq), axis=-1, keepdims=True)
    ms = jnp.where(first_head, s0, s1) * (1.0 / HEAD_DIM)
    return o * lax.rsqrt(ms + EPS)


def _project_steps(x_ref, g_ref, w_ref, rows, ph_ref, pa_ref):
    h = []
    n_h = HGRN_GROUPS * GROUP

    def piece(c0):
        if not h:
            h.append(_rms(x_ref[rows, :], g_ref[...]).astype(BF16))
        r = jnp.dot(h[0], w_ref[:, c0:c0 + PROJ_COLS], preferred_element_type=F32)
        if c0 < n_h:
            ph_ref[:, c0:c0 + PROJ_COLS] = r
        else:
            if c0 - n_h < GROUP:
                r = r * (1.0 / HEAD_DIM ** 0.5)
            pa_ref[rows, c0 - n_h:c0 - n_h + PROJ_COLS] = r.astype(BF16)

    return [functools.partial(piece, c0) for c0 in range(0, w_ref.shape[1], PROJ_COLS)]


def _interleave(first, second):
    for k in range(max(len(first), len(second))):
        if k < len(first):
            first[k]()
        if k < len(second):
            second[k]()


def _hgrn_steps(consts, ph_ref, ng_ref, st_ref, o_ref, out_row0):
    lb, cum, causal, same_head, first_head = consts
    n_chunks = ph_ref.shape[0] // CHUNK
    pair_cols = [slice(p * LANES, (p + 1) * LANES) for p in range(PAIRS)]
    val = {}

    def units(chunks):
        return [(n, p) for n in chunks for p in range(PAIRS)]

    def split_heads_rows(a):
        zero = jnp.zeros_like(a)
        return jnp.concatenate([jnp.where(first_head, a, zero),
                                jnp.where(first_head, zero, a)], axis=0)

    def group(n, c):
        return ph_ref[n * CHUNK:(n + 1) * CHUNK, c * GROUP:(c + 1) * GROUP]

    def log_decay(chunks):
        for n in chunks:
            f = lb + (1.0 - lb) * jax.nn.sigmoid(group(n, 1))
            logf = jnp.log(f)
            hi = logf.astype(BF16)
            lo = (logf - hi.astype(F32)).astype(BF16)
            val["k", n] = 1.0 - f
            val["b", n] = (jnp.dot(cum, hi, preferred_element_type=F32)
                           + jnp.dot(cum, lo, preferred_element_type=F32))

    def scaled_operands(chunks):
        for n in chunks:
            b, k = val.pop(("b", n)), val.pop(("k", n))
            b_end = b[CHUNK - 1:CHUNK, :]
            decay = jnp.exp(b_end)
            k_scaled = k * jnp.exp(-b)
            val["qe", n] = (group(n, 0) * jnp.exp(b)).astype(BF16)
            val["ke", n] = k_scaled.astype(BF16)
            val["ke_end", n] = (k_scaled * decay).astype(BF16)
            val["v", n] = group(n, 2).astype(BF16)
            val["decay", n] = decay

    def intra_scores(chunks):
        for n, p in units(chunks):
            qe, ke = val["qe", n][:, pair_cols[p]], val["ke", n][:, pair_cols[p]]
            scores = lax.dot_general(qe, split_heads_rows(ke), _NT, preferred_element_type=F32)
            val["scores", n, p] = jnp.where(causal, scores, 0.0).astype(BF16)

    def intra_out_and_update(chunks):
        for n, p in units(chunks):
            v, ke_end = val["v", n][:, pair_cols[p]], val["ke_end", n][:, pair_cols[p]]
            val["o", n, p] = jnp.dot(val.pop(("scores", n, p)), split_heads_rows(v),
                                     preferred_element_type=F32)
            kv_t = lax.dot_general(v, ke_end, _TN, preferred_element_type=F32)
            val["kv", n, p] = jnp.where(same_head, kv_t, 0.0)

    def carry_state(chunks):
        for n, p in units(chunks):
            st = st_ref[p]
            qe = val["qe", n][:, pair_cols[p]]
            val["o", n, p] = val["o", n, p] + lax.dot_general(
                qe, st.astype(BF16), _NT, preferred_element_type=F32)
            st_ref[p] = st * val["decay", n][:, pair_cols[p]] + val.pop(("kv", n, p))

    def write_out(chunks):
        for n in chunks:
            gate = group(n, 3)
            out_gain = ng_ref[...] * (gate * jax.nn.sigmoid(gate))
            out_rows = slice(out_row0 + n * CHUNK, out_row0 + (n + 1) * CHUNK)
            for p in range(PAIRS):
                o = val.pop(("o", n, p))
                o_ref[out_rows, pair_cols[p]] = (
                    _pair_rms(o, first_head) * out_gain[:, pair_cols[p]]).astype(o_ref.dtype)

    stages = [log_decay, scaled_operands, intra_scores, intra_out_and_update, carry_state, write_out]
    per_part = n_chunks // HGRN_STAGE_PARTS
    parts = [range(k * per_part, (k + 1) * per_part) for k in range(HGRN_STAGE_PARTS)]
    return [functools.partial(stage, part) for stage in stages for part in parts]


def _proj_hgrn_kernel(layer, tiles_per_seq, n_cast, x_ref, g_ref, w_ref, lbp_ref, ng_ref, *refs):
    cast_in, refs = refs[:n_cast], refs[n_cast:]
    pa_ref, o_ref = refs[:2]
    cast_out, (pha_ref, phb_ref, st_ref) = refs[2:2 + n_cast], refs[2 + n_cast:]
    for src, dst in zip(cast_in, cast_out):
        dst[...] = src[...].astype(dst.dtype)
    i = pl.program_id(0)
    tm = pha_ref.shape[0]
    lane = lax.broadcasted_iota(jnp.int32, (1, LANES), 1)
    first_head = lane < HEAD_DIM

    lbp = lbp_ref[...]
    ex = jnp.exp(lbp - jnp.max(lbp, axis=0, keepdims=True))
    sm = ex / jnp.sum(ex, axis=0, keepdims=True)
    lb = jnp.sum(sm[:layer + 1], axis=0, keepdims=True)

    r64 = lax.broadcasted_iota(jnp.int32, (CHUNK, CHUNK), 0)
    c64 = lax.broadcasted_iota(jnp.int32, (CHUNK, CHUNK), 1)
    cum = jnp.where(c64 <= r64, 1.0, 0.0).astype(BF16)
    row = lax.broadcasted_iota(jnp.int32, (CHUNK, LANES), 0)
    col = lax.broadcasted_iota(jnp.int32, (CHUNK, LANES), 1)
    causal = (col & (HEAD_DIM - 1)) <= row
    r128 = lax.broadcasted_iota(jnp.int32, (LANES, LANES), 0)
    c128 = lax.broadcasted_iota(jnp.int32, (LANES, LANES), 1)
    same_head = (r128 < HEAD_DIM) == (c128 < HEAD_DIM)
    consts = (lb, cum, causal, same_head, first_head)

    @pl.when(i == 0)
    def _first_step():
        phb_ref[...] = jnp.zeros_like(phb_ref)
        st_ref[...] = jnp.zeros_like(st_ref)

    _interleave(_project_steps(x_ref, g_ref, w_ref, slice(0, tm), pha_ref, pa_ref),
                _hgrn_steps(consts, phb_ref, ng_ref, st_ref, o_ref, 0))
    keep = jnp.where(lax.rem(2 * i, tiles_per_seq) == 0, 0.0, 1.0)
    st_ref[...] = st_ref[...] * keep
    _interleave(_project_steps(x_ref, g_ref, w_ref, slice(tm, 2 * tm), phb_ref, pa_ref),
                _hgrn_steps(consts, pha_ref, ng_ref, st_ref, o_ref, tm))


def _slab_spec(shape, steps):
    span = 1
    while shape[0] % (steps // span) or (shape[0] // (steps // span)) % BF16_SUBLANES:
        span *= 2
        assert span <= steps, shape
    rows = shape[0] // (steps // span)
    return pl.BlockSpec((rows, shape[1]), lambda i: (jnp.minimum(i, steps - 1) // span, 0))


def _proj_hgrn(x2d, gain, w_in, lower_bounds, norm_g, layer, seq, later_weights, tm=ROW_TILE):
    t, d = x2d.shape
    n_h, n_a = HGRN_GROUPS * GROUP, ATTN_GROUPS * GROUP
    n_lb = lower_bounds.shape[0]
    tiles_per_seq = seq // tm
    assert seq % tm == 0 and tiles_per_seq % 2 == 0
    steps = t // (2 * tm)
    last = steps - 1
    slabs = [_slab_spec(w.shape, steps) for w in later_weights]
    out = pl.pallas_call(
        functools.partial(_proj_hgrn_kernel, layer, tiles_per_seq, len(later_weights)),
        grid=(steps + 1,),
        in_specs=[pl.BlockSpec((2 * tm, d), lambda i: (jnp.minimum(i, last), 0)),
                  _resident((1, d)),
                  _resident((d, n_h + n_a)),
                  _resident((n_lb, GROUP)), _resident((1, GROUP))] + slabs,
        out_specs=[pl.BlockSpec((2 * tm, n_a), lambda i: (jnp.minimum(i, last), 0)),
                   pl.BlockSpec((2 * tm, GROUP), lambda i: (i, 0))] + slabs,
        out_shape=[jax.ShapeDtypeStruct((t, n_a), BF16),
                   jax.ShapeDtypeStruct((t + 2 * tm, GROUP), BF16)]
                  + [jax.ShapeDtypeStruct(w.shape, BF16) for w in later_weights],
        scratch_shapes=[pltpu.VMEM((tm, n_h), F32), pltpu.VMEM((tm, n_h), F32),
                        pltpu.VMEM((PAIRS, LANES, LANES), F32)],
        compiler_params=pltpu.CompilerParams(
            dimension_semantics=("arbitrary",), vmem_limit_bytes=VMEM_LIMIT),
        name="proj_hgrn2",
    )(x2d, gain, w_in, lower_bounds, norm_g, *later_weights)
    return out[0], out[1], out[2:]


def _attn_kernel(q_ref, k_ref, v_ref, tri_ref, ng_ref, o_ref, kk_ref, vv_ref, run_ref, acc_ref):
    qi = pl.program_id(1)
    n_blocks = k_ref.shape[1] // KEY_BLOCK
    diag_blocks = Q_BLOCK // KEY_BLOCK
    lane = lax.broadcasted_iota(jnp.int32, (1, LANES), 1)
    first_head = lane < HEAD_DIM
    streams = [(bi, slice(p * LANES, (p + 1) * LANES))
               for bi in range(q_ref.shape[0]) for p in range(PAIRS)]

    @pl.when(qi == 0)
    def _split_heads():
        dim_first_head = lax.broadcasted_iota(jnp.int32, (LANES, KEY_BLOCK), 0) < HEAD_DIM

        def body(j, carry):
            rows = pl.ds(pl.multiple_of(j * KEY_BLOCK, KEY_BLOCK), KEY_BLOCK)
            for s, (bi, cols) in enumerate(streams):
                kt = k_ref[bi, rows, cols].T
                vb = v_ref[bi, rows, cols]
                zero = jnp.zeros_like(vb)
                kk_ref[s, j, :, 0:KEY_BLOCK] = jnp.where(dim_first_head, kt, zero)
                kk_ref[s, j, :, KEY_BLOCK:2 * KEY_BLOCK] = jnp.where(dim_first_head, zero, kt)
                vv_ref[s, j, 0:KEY_BLOCK, :] = jnp.where(first_head, vb, zero)
                vv_ref[s, j, KEY_BLOCK:2 * KEY_BLOCK, :] = jnp.where(first_head, zero, vb)
            return carry
        lax.fori_loop(0, n_blocks, body, 0, unroll=2)

    mask_row = lax.broadcasted_iota(jnp.int32, (KEY_BLOCK, 2 * KEY_BLOCK), 0)
    mask_col = lax.broadcasted_iota(jnp.int32, (KEY_BLOCK, 2 * KEY_BLOCK), 1)
    strictly_before = (mask_col & (KEY_BLOCK - 1)) < mask_row

    def causal(a, diagonal):
        if not diagonal:
            return a
        top = jnp.where(strictly_before, a[:KEY_BLOCK], 0.0)
        return top if a.shape[0] == KEY_BLOCK else jnp.concatenate([top, a[KEY_BLOCK:]], axis=0)

    def scores(j, r0, r1, diagonal):
        zs, cs = [], []
        for s, (bi, cols) in enumerate(streams):
            q = q_ref[bi, r0:r1, cols]
            z = jnp.dot(q, kk_ref[s, j], preferred_element_type=F32)
            zb = z.astype(BF16)
            pos = jnp.maximum(zb, 0.0)
            neg = zb - pos
            c = causal(pos + jnp.log(1.0 + jnp.exp(neg - pos)), diagonal)
            zs.append(z)
            cs.append(c)
        incl_all = jnp.dot(jnp.concatenate(cs, axis=0), tri_ref[...], preferred_element_type=F32)
        return zs, incl_all

    def accumulate(j, r0, r1, diagonal, zs, incl_all):
        rows = r1 - r0

        def earlier(ref, s):
            if not diagonal:
                return ref[s, r0:r1]
            top = jnp.zeros((KEY_BLOCK, ref.shape[2]), F32)
            return top if rows == KEY_BLOCK else jnp.concatenate([top, ref[s, r0 + KEY_BLOCK:r1]], axis=0)

        for s in range(len(streams)):
            incl = incl_all[s * rows:(s + 1) * rows]
            run = earlier(run_ref, s)
            pr = causal(jnp.exp(zs[s] - incl - run), diagonal)
            pv = jnp.dot(pr.astype(BF16), vv_ref[s, j], preferred_element_type=F32)
            tot = jnp.concatenate(
                [jnp.broadcast_to(incl[:, 0:1], (rows, KEY_BLOCK)),
                 jnp.broadcast_to(incl[:, KEY_BLOCK:KEY_BLOCK + 1], (rows, KEY_BLOCK))], axis=1)
            run_ref[s, r0:r1] = run + tot
            acc_ref[s, r0:r1] = earlier(acc_ref, s) + pv

    def block(j, r0, r1, diagonal):
        accumulate(j, r0, r1, diagonal, *scores(j, r0, r1, diagonal))

    def tiles_needed(lowest=0):
        need = jnp.int32(lowest)
        for t in range(lowest, diag_blocks):
            live = jnp.min(functools.reduce(jnp.minimum, [
                run_ref[s, t * KEY_BLOCK:(t + 1) * KEY_BLOCK] for s in range(len(streams))]))
            need = jnp.where(live < RUN_CUTOFF, t + 1, need)
        return need

    def limited_block(j, r0, need, least, diagonal):
        branches = [functools.partial(block, r0=r0, r1=n * KEY_BLOCK, diagonal=diagonal)
                    for n in range(least, diag_blocks + 1)]
        lax.switch(need - least, branches, j)

    first_key_block = qi * diag_blocks
    always = [d for d in reversed(range(diag_blocks)) if d + DIAG_ALWAYS_TILES >= diag_blocks]
    staged = [scores(first_key_block + d, d * KEY_BLOCK, Q_BLOCK, True) for d in always]
    for d, stage in zip(always, staged):
        accumulate(first_key_block + d, d * KEY_BLOCK, Q_BLOCK, True, *stage)
    for d in reversed(range(diag_blocks)):
        least = d + DIAG_ALWAYS_TILES
        if least < diag_blocks:
            limited_block(first_key_block + d, d * KEY_BLOCK, tiles_needed(least), least, True)

    def more_blocks(state):
        t, need = state
        return jnp.logical_and(t < first_key_block, need > 0)

    def off_diagonal(state):
        t, need = state
        limited_block(first_key_block - 1 - t, 0, need, 1, False)
        return t + 1, tiles_needed()

    lax.while_loop(more_blocks, off_diagonal, (jnp.int32(0), tiles_needed()))
    for s, (bi, cols) in enumerate(streams):
        o_ref[bi, :, cols] = (_pair_rms(acc_ref[s], first_head) * ng_ref[:, cols]).astype(o_ref.dtype)


def _attention(proj_a, tri, norm_g):
    b, s, _ = proj_a.shape
    n_blocks = s // KEY_BLOCK
    assert b % ATTN_ROWS == 0
    n_streams = ATTN_ROWS * PAIRS
    return pl.pallas_call(
        _attn_kernel,
        grid=(b // ATTN_ROWS, s // Q_BLOCK),
        in_specs=[pl.BlockSpec((ATTN_ROWS, Q_BLOCK, GROUP), lambda bi, i: (bi, i, 0)),
                  pl.BlockSpec((ATTN_ROWS, s, GROUP), lambda bi, i: (bi, 0, 1)),
                  pl.BlockSpec((ATTN_ROWS, s, GROUP), lambda bi, i: (bi, 0, 2)),
                  _resident((2 * KEY_BLOCK, 2 * KEY_BLOCK)),
                  _resident((1, GROUP))],
        out_specs=pl.BlockSpec((ATTN_ROWS, Q_BLOCK, GROUP), lambda bi, i: (bi, i, 0)),
        out_shape=jax.ShapeDtypeStruct((b, s, GROUP), BF16),
        scratch_shapes=[pltpu.VMEM((n_streams, n_blocks, LANES, 2 * KEY_BLOCK), BF16),
                        pltpu.VMEM((n_streams, n_blocks, 2 * KEY_BLOCK, LANES), BF16),
                        pltpu.VMEM((n_streams, Q_BLOCK, 2 * KEY_BLOCK), F32),
                        pltpu.VMEM((n_streams, Q_BLOCK, LANES), F32)],
        compiler_params=pltpu.CompilerParams(
            dimension_semantics=("arbitrary", "arbitrary"), vmem_limit_bytes=ATTN_VMEM_LIMIT),
        name="stickbreak_attn",
    )(proj_a, proj_a, proj_a, tri, norm_g)


def _suffix_sum_matrix():
    r = jnp.arange(2 * KEY_BLOCK)[:, None]
    c = jnp.arange(2 * KEY_BLOCK)[None, :]
    return ((r >= c) & ((r < KEY_BLOCK) == (c < KEY_BLOCK))).astype(BF16)


def _out_ffn_kernel(final, x_ref, ma0_ref, ma1_ref, mb_ref, wo_ref, g2_ref, wg_ref, wu_ref, wd_ref,
                    g3_ref, o_ref, ff_ref):
    tm = ma0_ref.shape[0]
    tiles = [(slice(0, tm), ma0_ref), (slice(tm, 2 * tm), ma1_ref)]
    for rows, ma_ref in tiles:
        o_ref[rows, :] = (x_ref[rows, :]
                          + jnp.dot(ma_ref[...], wo_ref[0:GROUP, :], preferred_element_type=F32)
                          + jnp.dot(mb_ref[rows, :], wo_ref[GROUP:2 * GROUP, :],
                                    preferred_element_type=F32))
    hs = [_rms(o_ref[rows, :], g2_ref[...]).astype(BF16) for rows, _ in tiles]
    d_ff = wg_ref.shape[1]
    for c in range(d_ff // FF_CHUNK):
        cols = slice(c * FF_CHUNK, (c + 1) * FF_CHUNK)
        for (rows, _), h in zip(tiles, hs):
            gate = jnp.dot(h, wg_ref[:, cols], preferred_element_type=F32)
            up = jnp.dot(h, wu_ref[:, cols], preferred_element_type=F32)
            ff_ref[rows, cols] = (gate * jax.nn.sigmoid(gate) * up).astype(BF16)
    for rows, _ in tiles:
        x2 = o_ref[rows, :] + jnp.dot(ff_ref[rows, :], wd_ref[...], preferred_element_type=F32)
        o_ref[rows, :] = _rms(x2, g3_ref[...]) if final else x2


def _out_ffn(x2d, mix_a, mix_b, w_out, g2, w_gate, w_up, w_down, g3, final, tm=ROW_TILE):
    t, d = x2d.shape
    d_ff = w_gate.shape[1]
    assert t % (2 * tm) == 0
    row = lambda n: pl.BlockSpec((2 * tm, n), lambda i: (i, 0))
    return pl.pallas_call(
        functools.partial(_out_ffn_kernel, final),
        grid=(t // (2 * tm),),
        in_specs=[row(d),
                  pl.BlockSpec((tm, GROUP), lambda i: (2 * i + 1, 0)),
                  pl.BlockSpec((tm, GROUP), lambda i: (2 * i + 2, 0)),
                  row(GROUP),
                  _resident((2 * GROUP, d)), _resident((1, d)),
                  _resident((d, d_ff)), _resident((d, d_ff)), _resident((d_ff, d)),
                  _resident((1, d))],
        out_specs=row(d),
        out_shape=jax.ShapeDtypeStruct((t, d), F32),
        scratch_shapes=[pltpu.VMEM((2 * tm, d_ff), BF16)],
        compiler_params=pltpu.CompilerParams(
            dimension_semantics=("arbitrary",), vmem_limit_bytes=ATTN_VMEM_LIMIT),
        name="out_ffn",
    )(x2d, mix_a, mix_a, mix_b, w_out, g2, w_gate, w_up, w_down, g3)


def kernel(x, mix_norm_g, w_in, lower_bounds, hgrn_norm_g, sb_norm_g, w_out, ffn_norm_g,
           w_gate, w_up, w_down, final_norm_g):
    b, s, d = x.shape
    depth = w_in.shape[0]
    assert w_in.shape[2] == (HGRN_GROUPS + ATTN_GROUPS) * GROUP
    assert w_gate.shape[2] % FF_CHUNK == 0 and s % KEY_BLOCK == 0
    tri = _suffix_sum_matrix()
    x2d = x.reshape(b * s, d)
    for l in range(depth):
        proj_a, mix_a, (wo, wg, wu, wd) = _proj_hgrn(
            x2d, mix_norm_g[l][None], w_in[l].astype(BF16), lower_bounds, hgrn_norm_g[l][None], l, s,
            (w_out[l], w_gate[l], w_up[l], w_down[l]))
        mix_b = _attention(proj_a.reshape(b, s, -1), tri, sb_norm_g[l][None])
        last = l == depth - 1
        g3 = final_norm_g[None] if last else ffn_norm_g[l][None]
        x2d = _out_ffn(x2d, mix_a, mix_b.reshape(b * s, -1), wo, ffn_norm_g[l][None], wg, wu, wd,
                       g3, last)
    if depth == 0:
        raise ValueError("depth must be positive")
    return x2d.reshape(b, s, d)
```

```python
import functools

import jax
import jax.numpy as jnp
from jax import lax
from jax.experimental import pallas as pl
from jax.experimental.pallas import tpu as pltpu

F32 = jnp.float32
BF16 = jnp.bfloat16

LANES = 128
BF16_SUBLANES = 16
FF_CHUNK = 256
PROJ_COLS = 256
HGRN_STAGE_PARTS = 2
HEAD_DIM = 64
HEADS_PER_GROUP = 8
GROUP = HEADS_PER_GROUP * HEAD_DIM
PAIRS = GROUP // LANES
HGRN_GROUPS = 4
ATTN_GROUPS = 3
CHUNK = 64
ROW_TILE = 512
KEY_BLOCK = 128
Q_BLOCK = 512
ATTN_ROWS = 2
DIAG_ALWAYS_TILES = 3
RUN_CUTOFF = 95.0
EPS = 1e-6
VMEM_LIMIT = 56 * 1024 * 1024
ATTN_VMEM_LIMIT = 60 * 1024 * 1024

_NT = (((1,), (1,)), ((), ()))
_TN = (((0,), (0,)), ((), ()))


def _resident(shape):
    return pl.BlockSpec(shape, lambda *_: (0,) * len(shape), pipeline_mode=pl.Buffered(1))


def _rms(x, gain):
    return x * lax.rsqrt(jnp.mean(x * x, axis=-1, keepdims=True) + EPS) * gain


def _pair_rms(o, first_head):
    sq = o * o
    s0 = jnp.sum(jnp.where(first_head, sq, 0.0), axis=-1, keepdims=True)
    s1 = jnp.sum(jnp.where(first_head, 0.0, sq), axis=-1, keepdims=True)
    ms = jnp.where(first_head, s0, s1) * (1.0 / HEAD_DIM)
    return o * lax.rsqrt(ms + EPS)


def _project_steps(x_ref, g_ref, w_ref, rows, ph_ref, pa_ref):
    h = []
    n_h = HGRN_GROUPS * GROUP

    def piece(c0):
        if not h:
            h.append(_rms(x_ref[rows, :], g_ref[...]).astype(BF16))
        r = jnp.dot(h[0], w_ref[:, c0:c0 + PROJ_COLS], preferred_element_type=F32)
        if c0 < n_h:
            ph_ref[:, c0:c0 + PROJ_COLS] = r
        else:
            if c0 - n_h < GROUP:
                r = r * (1.0 / HEAD_DIM ** 0.5)
            pa_ref[rows, c0 - n_h:c0 - n_h + PROJ_COLS] = r.astype(BF16)

    return [functools.partial(piece, c0) for c0 in range(0, w_ref.shape[1], PROJ_COLS)]


def _interleave(first, second):
    for k in range(max(len(first), len(second))):
        if k < len(first):
            first[k]()
        if k < len(second):
            second[k]()


def _hgrn_steps(consts, ph_ref, ng_ref, st_ref, o_ref, out_row0):
    lb, cum, causal, same_head, first_head = consts
    n_chunks = ph_ref.shape[0] // CHUNK
    pair_cols = [slice(p * LANES, (p + 1) * LANES) for p in range(PAIRS)]
    val = {}

    def units(chunks):
        return [(n, p) for n in chunks for p in range(PAIRS)]

    def split_heads_rows(a):
        zero = jnp.zeros_like(a)
        return jnp.concatenate([jnp.where(first_head, a, zero),
                                jnp.where(first_head, zero, a)], axis=0)

    def group(n, c):
        return ph_ref[n * CHUNK:(n + 1) * CHUNK, c * GROUP:(c + 1) * GROUP]

    def log_decay(chunks):
        for n in chunks:
            f = lb + (1.0 - lb) * jax.nn.sigmoid(group(n, 1))
            logf = jnp.log(f)
            hi = logf.astype(BF16)
            lo = (logf - hi.astype(F32)).astype(BF16)
            val["k", n] = 1.0 - f
            val["b", n] = jnp.dot(cum, jnp.concatenate([hi, lo], axis=0),
                                  preferred_element_type=F32)

    def scaled_operands(chunks):
        for n in chunks:
            b, k = val.pop(("b", n)), val.pop(("k", n))
            b_end = b[CHUNK - 1:CHUNK, :]
            decay = jnp.exp(b_end)
            k_scaled = k * jnp.exp(-b)
            val["qe", n] = (group(n, 0) * jnp.exp(b)).astype(BF16)
            val["ke", n] = k_scaled.astype(BF16)
            val["ke_end", n] = (k_scaled * decay).astype(BF16)
            val["v", n] = group(n, 2).astype(BF16)
            val["decay", n] = decay

    def intra_scores(chunks):
        for n, p in units(chunks):
            qe, ke = val["qe", n][:, pair_cols[p]], val["ke", n][:, pair_cols[p]]
            scores = lax.dot_general(qe, split_heads_rows(ke), _NT, preferred_element_type=F32)
            val["scores", n, p] = jnp.where(causal, scores, 0.0).astype(BF16)

    def intra_out_and_update(chunks):
        for n, p in units(chunks):
            v, ke_end = val["v", n][:, pair_cols[p]], val["ke_end", n][:, pair_cols[p]]
            val["o", n, p] = jnp.dot(val.pop(("scores", n, p)), split_heads_rows(v),
                                     preferred_element_type=F32)
            kv_t = lax.dot_general(v, ke_end, _TN, preferred_element_type=F32)
            val["kv", n, p] = jnp.where(same_head, kv_t, 0.0)

    def carry_state(chunks):
        for n, p in units(chunks):
            st = st_ref[p]
            qe = val["qe", n][:, pair_cols[p]]
            val["o", n, p] = val["o", n, p] + lax.dot_general(
                qe, st.astype(BF16), _NT, preferred_element_type=F32)
            st_ref[p] = st * val["decay", n][:, pair_cols[p]] + val.pop(("kv", n, p))

    def write_out(chunks):
        for n in chunks:
            gate = group(n, 3)
            out_gain = ng_ref[...] * (gate * jax.nn.sigmoid(gate))
            out_rows = slice(out_row0 + n * CHUNK, out_row0 + (n + 1) * CHUNK)
            for p in range(PAIRS):
                o = val.pop(("o", n, p))
                o_ref[out_rows, pair_cols[p]] = (
                    _pair_rms(o, first_head) * out_gain[:, pair_cols[p]]).astype(o_ref.dtype)

    stages = [log_decay, scaled_operands, intra_scores, intra_out_and_update, carry_state, write_out]
    per_part = n_chunks // HGRN_STAGE_PARTS
    parts = [range(k * per_part, (k + 1) * per_part) for k in range(HGRN_STAGE_PARTS)]
    return [functools.partial(stage, part) for stage in stages for part in parts]


def _proj_hgrn_kernel(layer, tiles_per_seq, n_cast, x_ref, g_ref, w_ref, lbp_ref, ng_ref, *refs):
    cast_in, refs = refs[:n_cast], refs[n_cast:]
    pa_ref, o_ref = refs[:2]
    cast_out, (pha_ref, phb_ref, st_ref) = refs[2:2 + n_cast], refs[2 + n_cast:]
    for src, dst in zip(cast_in, cast_out):
        dst[...] = src[...].astype(dst.dtype)
    i = pl.program_id(0)
    tm = pha_ref.shape[0]
    lane = lax.broadcasted_iota(jnp.int32, (1, LANES), 1)
    first_head = lane < HEAD_DIM

    lbp = lbp_ref[...]
    ex = jnp.exp(lbp - jnp.max(lbp, axis=0, keepdims=True))
    sm = ex / jnp.sum(ex, axis=0, keepdims=True)
    lb = jnp.sum(sm[:layer + 1], axis=0, keepdims=True)

    r64 = lax.broadcasted_iota(jnp.int32, (CHUNK, 2 * CHUNK), 0)
    c64 = lax.broadcasted_iota(jnp.int32, (CHUNK, 2 * CHUNK), 1)
    cum = jnp.where((c64 & (CHUNK - 1)) <= r64, 1.0, 0.0).astype(BF16)
    row = lax.broadcasted_iota(jnp.int32, (CHUNK, LANES), 0)
    col = lax.broadcasted_iota(jnp.int32, (CHUNK, LANES), 1)
    causal = (col & (HEAD_DIM - 1)) <= row
    r128 = lax.broadcasted_iota(jnp.int32, (LANES, LANES), 0)
    c128 = lax.broadcasted_iota(jnp.int32, (LANES, LANES), 1)
    same_head = (r128 < HEAD_DIM) == (c128 < HEAD_DIM)
    consts = (lb, cum, causal, same_head, first_head)

    @pl.when(i == 0)
    def _first_step():
        phb_ref[...] = jnp.zeros_like(phb_ref)
        st_ref[...] = jnp.zeros_like(st_ref)

    _interleave(_project_steps(x_ref, g_ref, w_ref, slice(0, tm), pha_ref, pa_ref),
                _hgrn_steps(consts, phb_ref, ng_ref, st_ref, o_ref, 0))
    keep = jnp.where(lax.rem(2 * i, tiles_per_seq) == 0, 0.0, 1.0)
    st_ref[...] = st_ref[...] * keep
    _interleave(_project_steps(x_ref, g_ref, w_ref, slice(tm, 2 * tm), phb_ref, pa_ref),
                _hgrn_steps(consts, pha_ref, ng_ref, st_ref, o_ref, tm))


def _slab_spec(shape, steps):
    span = 1
    while shape[0] % (steps // span) or (shape[0] // (steps // span)) % BF16_SUBLANES:
        span *= 2
        assert span <= steps, shape
    rows = shape[0] // (steps // span)
    return pl.BlockSpec((rows, shape[1]), lambda i: (jnp.minimum(i, steps - 1) // span, 0))


def _proj_hgrn(x2d, gain, w_in, lower_bounds, norm_g, layer, seq, later_weights, tm=ROW_TILE):
    t, d = x2d.shape
    n_h, n_a = HGRN_GROUPS * GROUP, ATTN_GROUPS * GROUP
    n_lb = lower_bounds.shape[0]
    tiles_per_seq = seq // tm
    assert seq % tm == 0 and tiles_per_seq % 2 == 0
    steps = t // (2 * tm)
    last = steps - 1
    slabs = [_slab_spec(w.shape, steps) for w in later_weights]
    out = pl.pallas_call(
        functools.partial(_proj_hgrn_kernel, layer, tiles_per_seq, len(later_weights)),
        grid=(steps + 1,),
        in_specs=[pl.BlockSpec((2 * tm, d), lambda i: (jnp.minimum(i, last), 0)),
                  _resident((1, d)),
                  _resident((d, n_h + n_a)),
                  _resident((n_lb, GROUP)), _resident((1, GROUP))] + slabs,
        out_specs=[pl.BlockSpec((2 * tm, n_a), lambda i: (jnp.minimum(i, last), 0)),
                   pl.BlockSpec((2 * tm, GROUP), lambda i: (i, 0))] + slabs,
        out_shape=[jax.ShapeDtypeStruct((t, n_a), BF16),
                   jax.ShapeDtypeStruct((t + 2 * tm, GROUP), BF16)]
                  + [jax.ShapeDtypeStruct(w.shape, BF16) for w in later_weights],
        scratch_shapes=[pltpu.VMEM((tm, n_h), F32), pltpu.VMEM((tm, n_h), F32),
                        pltpu.VMEM((PAIRS, LANES, LANES), F32)],
        compiler_params=pltpu.CompilerParams(
            dimension_semantics=("arbitrary",), vmem_limit_bytes=VMEM_LIMIT),
        name="proj_hgrn2",
    )(x2d, gain, w_in, lower_bounds, norm_g, *later_weights)
    return out[0], out[1], out[2:]


def _attn_kernel(q_ref, k_ref, v_ref, tri_ref, ng_ref, o_ref, kk_ref, vv_ref, run_ref, acc_ref):
    qi = pl.program_id(1)
    n_blocks = k_ref.shape[1] // KEY_BLOCK
    diag_blocks = Q_BLOCK // KEY_BLOCK
    lane = lax.broadcasted_iota(jnp.int32, (1, LANES), 1)
    first_head = lane < HEAD_DIM
    streams = [(bi, slice(p * LANES, (p + 1) * LANES))
               for bi in range(q_ref.shape[0]) for p in range(PAIRS)]

    @pl.when(qi == 0)
    def _split_heads():
        dim_first_head = lax.broadcasted_iota(jnp.int32, (LANES, KEY_BLOCK), 0) < HEAD_DIM

        def body(j, carry):
            rows = pl.ds(pl.multiple_of(j * KEY_BLOCK, KEY_BLOCK), KEY_BLOCK)
            for s, (bi, cols) in enumerate(streams):
                kt = k_ref[bi, rows, cols].T
                vb = v_ref[bi, rows, cols]
                zero = jnp.zeros_like(vb)
                kk_ref[s, j, :, 0:KEY_BLOCK] = jnp.where(dim_first_head, kt, zero)
                kk_ref[s, j, :, KEY_BLOCK:2 * KEY_BLOCK] = jnp.where(dim_first_head, zero, kt)
                vv_ref[s, j, 0:KEY_BLOCK, :] = jnp.where(first_head, vb, zero)
                vv_ref[s, j, KEY_BLOCK:2 * KEY_BLOCK, :] = jnp.where(first_head, zero, vb)
            return carry
        lax.fori_loop(0, n_blocks, body, 0, unroll=2)

    mask_row = lax.broadcasted_iota(jnp.int32, (KEY_BLOCK, 2 * KEY_BLOCK), 0)
    mask_col = lax.broadcasted_iota(jnp.int32, (KEY_BLOCK, 2 * KEY_BLOCK), 1)
    strictly_before = (mask_col & (KEY_BLOCK - 1)) < mask_row

    def causal(a, diagonal):
        if not diagonal:
            return a
        top = jnp.where(strictly_before, a[:KEY_BLOCK], 0.0)
        return top if a.shape[0] == KEY_BLOCK else jnp.concatenate([top, a[KEY_BLOCK:]], axis=0)

    def scores(j, r0, r1, diagonal):
        zs, cs = [], []
        for s, (bi, cols) in enumerate(streams):
            q = q_ref[bi, r0:r1, cols]
            z = jnp.dot(q, kk_ref[s, j], preferred_element_type=F32)
            zb = z.astype(BF16)
            pos = jnp.maximum(zb, 0.0)
            neg = zb - pos
            c = causal(pos + jnp.log(1.0 + jnp.exp(neg - pos)), diagonal)
            zs.append(z)
            cs.append(c)
        incl_all = jnp.dot(jnp.concatenate(cs, axis=0), tri_ref[...], preferred_element_type=F32)
        return zs, incl_all

    def accumulate(j, r0, r1, diagonal, zs, incl_all):
        rows = r1 - r0

        def earlier(ref, s):
            if not diagonal:
                return ref[s, r0:r1]
            top = jnp.zeros((KEY_BLOCK, ref.shape[2]), F32)
            return top if rows == KEY_BLOCK else jnp.concatenate([top, ref[s, r0 + KEY_BLOCK:r1]], axis=0)

        for s in range(len(streams)):
            incl = incl_all[s * rows:(s + 1) * rows]
            run = earlier(run_ref, s)
            pr = causal(jnp.exp(zs[s] - incl - run), diagonal)
            pv = jnp.dot(pr.astype(BF16), vv_ref[s, j], preferred_element_type=F32)
            tot = jnp.concatenate(
                [jnp.broadcast_to(incl[:, 0:1], (rows, KEY_BLOCK)),
                 jnp.broadcast_to(incl[:, KEY_BLOCK:KEY_BLOCK + 1], (rows, KEY_BLOCK))], axis=1)
            run_ref[s, r0:r1] = run + tot
            acc_ref[s, r0:r1] = earlier(acc_ref, s) + pv

    def block(j, r0, r1, diagonal):
        accumulate(j, r0, r1, diagonal, *scores(j, r0, r1, diagonal))

    def tiles_needed(lowest=0):
        need = jnp.int32(lowest)
        for t in range(lowest, diag_blocks):
            live = jnp.min(functools.reduce(jnp.minimum, [
                run_ref[s, t * KEY_BLOCK:(t + 1) * KEY_BLOCK] for s in range(len(streams))]))
            need = jnp.where(live < RUN_CUTOFF, t + 1, need)
        return need

    def limited_block(j, r0, need, least, diagonal):
        branches = [functools.partial(block, r0=r0, r1=n * KEY_BLOCK, diagonal=diagonal)
                    for n in range(least, diag_blocks + 1)]
        lax.switch(need - least, branches, j)

    first_key_block = qi * diag_blocks
    always = [d for d in reversed(range(diag_blocks)) if d + DIAG_ALWAYS_TILES >= diag_blocks]
    staged = [scores(first_key_block + d, d * KEY_BLOCK, Q_BLOCK, True) for d in always]
    for d, stage in zip(always, staged):
        accumulate(first_key_block + d, d * KEY_BLOCK, Q_BLOCK, True, *stage)
    for d in reversed(range(diag_blocks)):
        least = d + DIAG_ALWAYS_TILES
        if least < diag_blocks:
            limited_block(first_key_block + d, d * KEY_BLOCK, tiles_needed(least), least, True)

    def more_blocks(state):
        t, need = state
        return jnp.logical_and(t < first_key_block, need > 0)

    def off_diagonal(state):
        t, need = state
        limited_block(first_key_block - 1 - t, 0, need, 1, False)
        return t + 1, tiles_needed()

    lax.while_loop(more_blocks, off_diagonal, (jnp.int32(0), tiles_needed()))
    for s, (bi, cols) in enumerate(streams):
        o_ref[bi, :, cols] = (_pair_rms(acc_ref[s], first_head) * ng_ref[:, cols]).astype(o_ref.dtype)


def _attention(proj_a, tri, norm_g):
    b, s, _ = proj_a.shape
    n_blocks = s // KEY_BLOCK
    assert b % ATTN_ROWS == 0
    n_streams = ATTN_ROWS * PAIRS
    return pl.pallas_call(
        _attn_kernel,
        grid=(b // ATTN_ROWS, s // Q_BLOCK),
        in_specs=[pl.BlockSpec((ATTN_ROWS, Q_BLOCK, GROUP), lambda bi, i: (bi, i, 0)),
                  pl.BlockSpec((ATTN_ROWS, s, GROUP), lambda bi, i: (bi, 0, 1)),
                  pl.BlockSpec((ATTN_ROWS, s, GROUP), lambda bi, i: (bi, 0, 2)),
                  _resident((2 * KEY_BLOCK, 2 * KEY_BLOCK)),
                  _resident((1, GROUP))],
        out_specs=pl.BlockSpec((ATTN_ROWS, Q_BLOCK, GROUP), lambda bi, i: (bi, i, 0)),
        out_shape=jax.ShapeDtypeStruct((b, s, GROUP), BF16),
        scratch_shapes=[pltpu.VMEM((n_streams, n_blocks, LANES, 2 * KEY_BLOCK), BF16),
                        pltpu.VMEM((n_streams, n_blocks, 2 * KEY_BLOCK, LANES), BF16),
                        pltpu.VMEM((n_streams, Q_BLOCK, 2 * KEY_BLOCK), F32),
                        pltpu.VMEM((n_streams, Q_BLOCK, LANES), F32)],
        compiler_params=pltpu.CompilerParams(
            dimension_semantics=("arbitrary", "arbitrary"), vmem_limit_bytes=ATTN_VMEM_LIMIT),
        name="stickbreak_attn",
    )(proj_a, proj_a, proj_a, tri, norm_g)


def _suffix_sum_matrix():
    r = jnp.arange(2 * KEY_BLOCK)[:, None]
    c = jnp.arange(2 * KEY_BLOCK)[None, :]
    return ((r >= c) & ((r < KEY_BLOCK) == (c < KEY_BLOCK))).astype(BF16)


def _out_ffn_kernel(final, x_ref, ma0_ref, ma1_ref, mb_ref, wo_ref, g2_ref, wg_ref, wu_ref, wd_ref,
                    g3_ref, o_ref, ff_ref):
    tm = ma0_ref.shape[0]
    tiles = [(slice(0, tm), ma0_ref), (slice(tm, 2 * tm), ma1_ref)]
    for rows, ma_ref in tiles:
        o_ref[rows, :] = (x_ref[rows, :]
                          + jnp.dot(ma_ref[...], wo_ref[0:GROUP, :], preferred_element_type=F32)
                          + jnp.dot(mb_ref[rows, :], wo_ref[GROUP:2 * GROUP, :],
                                    preferred_element_type=F32))
    hs = [_rms(o_ref[rows, :], g2_ref[...]).astype(BF16) for rows, _ in tiles]
    d_ff = wg_ref.shape[1]
    for c in range(d_ff // FF_CHUNK):
        cols = slice(c * FF_CHUNK, (c + 1) * FF_CHUNK)
        for (rows, _), h in zip(tiles, hs):
            gate = jnp.dot(h, wg_ref[:, cols], preferred_element_type=F32)
            up = jnp.dot(h, wu_ref[:, cols], preferred_element_type=F32)
            ff_ref[rows, cols] = (gate * jax.nn.sigmoid(gate) * up).astype(BF16)
    for rows, _ in tiles:
        x2 = o_ref[rows, :] + jnp.dot(ff_ref[rows, :], wd_ref[...], preferred_element_type=F32)
        o_ref[rows, :] = _rms(x2, g3_ref[...]) if final else x2


def _out_ffn(x2d, mix_a, mix_b, w_out, g2, w_gate, w_up, w_down, g3, final, tm=ROW_TILE):
    t, d = x2d.shape
    d_ff = w_gate.shape[1]
    assert t % (2 * tm) == 0
    row = lambda n: pl.BlockSpec((2 * tm, n), lambda i: (i, 0))
    return pl.pallas_call(
        functools.partial(_out_ffn_kernel, final),
        grid=(t // (2 * tm),),
        in_specs=[row(d),
                  pl.BlockSpec((tm, GROUP), lambda i: (2 * i + 1, 0)),
                  pl.BlockSpec((tm, GROUP), lambda i: (2 * i + 2, 0)),
                  row(GROUP),
                  _resident((2 * GROUP, d)), _resident((1, d)),
                  _resident((d, d_ff)), _resident((d, d_ff)), _resident((d_ff, d)),
                  _resident((1, d))],
        out_specs=row(d),
        out_shape=jax.ShapeDtypeStruct((t, d), F32),
        scratch_shapes=[pltpu.VMEM((2 * tm, d_ff), BF16)],
        compiler_params=pltpu.CompilerParams(
            dimension_semantics=("arbitrary",), vmem_limit_bytes=ATTN_VMEM_LIMIT),
        name="out_ffn",
    )(x2d, mix_a, mix_a, mix_b, w_out, g2, w_gate, w_up, w_down, g3)


def kernel(x, mix_norm_g, w_in, lower_bounds, hgrn_norm_g, sb_norm_g, w_out, ffn_norm_g,
           w_gate, w_up, w_down, final_norm_g):
    b, s, d = x.shape
    depth = w_in.shape[0]
    assert w_in.shape[2] == (HGRN_GROUPS + ATTN_GROUPS) * GROUP
    assert w_gate.shape[2] % FF_CHUNK == 0 and s % KEY_BLOCK == 0
    tri = _suffix_sum_matrix()
    x2d = x.reshape(b * s, d)
    for l in range(depth):
        proj_a, mix_a, (wo, wg, wu, wd) = _proj_hgrn(
            x2d, mix_norm_g[l][None], w_in[l].astype(BF16), lower_bounds, hgrn_norm_g[l][None], l, s,
            (w_out[l], w_gate[l], w_up[l], w_down[l]))
        mix_b = _attention(proj_a.reshape(b, s, -1), tri, sb_norm_g[l][None])
        last = l == depth - 1
        g3 = final_norm_g[None] if last else ffn_norm_g[l][None]
        x2d = _out_ffn(x2d, mix_a, mix_b.reshape(b * s, -1), wo, ffn_norm_g[l][None], wg, wu, wd,
                       g3, last)
    if depth == 0:
        raise ValueError("depth must be positive")
    return x2d.reshape(b, s, d)
```

```python
import functools

import jax
import jax.numpy as jnp
from jax import lax
from jax.experimental import pallas as pl
from jax.experimental.pallas import tpu as pltpu

F32 = jnp.float32
BF16 = jnp.bfloat16

LANES = 128
BF16_SUBLANES = 16
FF_CHUNK = 256
PROJ_COLS = 256
HGRN_STAGE_PARTS = 2
HEAD_DIM = 64
HEADS_PER_GROUP = 8
GROUP = HEADS_PER_GROUP * HEAD_DIM
PAIRS = GROUP // LANES
HGRN_GROUPS = 4
ATTN_GROUPS = 3
CHUNK = 64
ROW_TILE = 512
KEY_BLOCK = 128
Q_BLOCK = 512
ATTN_ROWS = 2
DIAG_ALWAYS_TILES = 3
RUN_CUTOFF = 95.0
EPS = 1e-6
VMEM_LIMIT = 56 * 1024 * 1024
ATTN_VMEM_LIMIT = 60 * 1024 * 1024

_NT = (((1,), (1,)), ((), ()))
_TN = (((0,), (0,)), ((), ()))


def _resident(shape):
    return pl.BlockSpec(shape, lambda *_: (0,) * len(shape), pipeline_mode=pl.Buffered(1))


def _rms(x, gain):
    return x * lax.rsqrt(jnp.mean(x * x, axis=-1, keepdims=True) + EPS) * gain


def _pair_rms(o, first_head):
    sq = o * o
    s0 = jnp.sum(jnp.where(first_head, sq, 0.0), axis=-1, keepdims=True)
    s1 = jnp.sum(jnp.where(first_head, 0.0, sq), axis=-1, keepdims=True)
    ms = jnp.where(first_head, s0, s1) * (1.0 / HEAD_DIM)
    return o * lax.rsqrt(ms + EPS)


def _project_steps(x_ref, g_ref, w_ref, rows, ph_ref, pa_ref):
    h = []
    n_h = HGRN_GROUPS * GROUP

    def piece(c0):
        if not h:
            h.append(_rms(x_ref[rows, :], g_ref[...]).astype(BF16))
        r = jnp.dot(h[0], w_ref[:, c0:c0 + PROJ_COLS].astype(BF16), preferred_element_type=F32)
        if c0 < n_h:
            ph_ref[:, c0:c0 + PROJ_COLS] = r
        else:
            if c0 - n_h < GROUP:
                r = r * (1.0 / HEAD_DIM ** 0.5)
            pa_ref[rows, c0 - n_h:c0 - n_h + PROJ_COLS] = r.astype(BF16)

    return [functools.partial(piece, c0) for c0 in range(0, w_ref.shape[1], PROJ_COLS)]


def _interleave(first, second):
    for k in range(max(len(first), len(second))):
        if k < len(first):
            first[k]()
        if k < len(second):
            second[k]()


def _hgrn_steps(consts, ph_ref, ng_ref, st_ref, o_ref, out_row0):
    lb, cum, causal, same_head, first_head = consts
    n_chunks = ph_ref.shape[0] // CHUNK
    pair_cols = [slice(p * LANES, (p + 1) * LANES) for p in range(PAIRS)]
    val = {}

    def units(chunks):
        return [(n, p) for n in chunks for p in range(PAIRS)]

    def split_heads_rows(a):
        zero = jnp.zeros_like(a)
        return jnp.concatenate([jnp.where(first_head, a, zero),
                                jnp.where(first_head, zero, a)], axis=0)

    def group(n, c):
        return ph_ref[n * CHUNK:(n + 1) * CHUNK, c * GROUP:(c + 1) * GROUP]

    def log_decay(chunks):
        for n in chunks:
            f = lb + (1.0 - lb) * jax.nn.sigmoid(group(n, 1))
            logf = jnp.log(f)
            hi = logf.astype(BF16)
            lo = (logf - hi.astype(F32)).astype(BF16)
            val["k", n] = 1.0 - f
            val["b", n] = jnp.dot(cum, jnp.concatenate([hi, lo], axis=0),
                                  preferred_element_type=F32)

    def scaled_operands(chunks):
        for n in chunks:
            b, k = val.pop(("b", n)), val.pop(("k", n))
            b_end = b[CHUNK - 1:CHUNK, :]
            decay = jnp.exp(b_end)
            k_scaled = k * jnp.exp(-b)
            val["qe", n] = (group(n, 0) * jnp.exp(b)).astype(BF16)
            val["ke", n] = k_scaled.astype(BF16)
            val["ke_end", n] = (k_scaled * decay).astype(BF16)
            val["v", n] = group(n, 2).astype(BF16)
            val["decay", n] = decay

    def intra_scores(chunks):
        for n, p in units(chunks):
            qe, ke = val["qe", n][:, pair_cols[p]], val["ke", n][:, pair_cols[p]]
            scores = lax.dot_general(qe, split_heads_rows(ke), _NT, preferred_element_type=F32)
            val["scores", n, p] = jnp.where(causal, scores, 0.0).astype(BF16)

    def intra_out_and_update(chunks):
        for n, p in units(chunks):
            v, ke_end = val["v", n][:, pair_cols[p]], val["ke_end", n][:, pair_cols[p]]
            val["o", n, p] = jnp.dot(val.pop(("scores", n, p)), split_heads_rows(v),
                                     preferred_element_type=F32)
            kv_t = lax.dot_general(v, ke_end, _TN, preferred_element_type=F32)
            val["kv", n, p] = jnp.where(same_head, kv_t, 0.0)

    def carry_state(chunks):
        for n, p in units(chunks):
            st = st_ref[p]
            qe = val["qe", n][:, pair_cols[p]]
            val["o", n, p] = val["o", n, p] + lax.dot_general(
                qe, st.astype(BF16), _NT, preferred_element_type=F32)
            st_ref[p] = st * val["decay", n][:, pair_cols[p]] + val.pop(("kv", n, p))

    def write_out(chunks):
        for n in chunks:
            gate = group(n, 3)
            out_gain = ng_ref[...] * (gate * jax.nn.sigmoid(gate))
            out_rows = slice(out_row0 + n * CHUNK, out_row0 + (n + 1) * CHUNK)
            for p in range(PAIRS):
                o = val.pop(("o", n, p))
                o_ref[out_rows, pair_cols[p]] = (
                    _pair_rms(o, first_head) * out_gain[:, pair_cols[p]]).astype(o_ref.dtype)

    stages = [log_decay, scaled_operands, intra_scores, intra_out_and_update, carry_state, write_out]
    per_part = n_chunks // HGRN_STAGE_PARTS
    parts = [range(k * per_part, (k + 1) * per_part) for k in range(HGRN_STAGE_PARTS)]
    return [functools.partial(stage, part) for stage in stages for part in parts]


def _proj_hgrn_kernel(layer, tiles_per_seq, n_cast, x_ref, g_ref, w_ref, lbp_ref, ng_ref, *refs):
    cast_in, refs = refs[:n_cast], refs[n_cast:]
    pa_ref, o_ref = refs[:2]
    cast_out, (pha_ref, phb_ref, st_ref) = refs[2:2 + n_cast], refs[2 + n_cast:]
    for src, dst in zip(cast_in, cast_out):
        dst[...] = src[...].astype(dst.dtype)
    i = pl.program_id(0)
    tm = pha_ref.shape[0]
    lane = lax.broadcasted_iota(jnp.int32, (1, LANES), 1)
    first_head = lane < HEAD_DIM

    lbp = lbp_ref[...]
    ex = jnp.exp(lbp - jnp.max(lbp, axis=0, keepdims=True))
    sm = ex / jnp.sum(ex, axis=0, keepdims=True)
    lb = jnp.sum(sm[:layer + 1], axis=0, keepdims=True)

    r64 = lax.broadcasted_iota(jnp.int32, (CHUNK, 2 * CHUNK), 0)
    c64 = lax.broadcasted_iota(jnp.int32, (CHUNK, 2 * CHUNK), 1)
    cum = jnp.where((c64 & (CHUNK - 1)) <= r64, 1.0, 0.0).astype(BF16)
    row = lax.broadcasted_iota(jnp.int32, (CHUNK, LANES), 0)
    col = lax.broadcasted_iota(jnp.int32, (CHUNK, LANES), 1)
    causal = (col & (HEAD_DIM - 1)) <= row
    r128 = lax.broadcasted_iota(jnp.int32, (LANES, LANES), 0)
    c128 = lax.broadcasted_iota(jnp.int32, (LANES, LANES), 1)
    same_head = (r128 < HEAD_DIM) == (c128 < HEAD_DIM)
    consts = (lb, cum, causal, same_head, first_head)

    @pl.when(i == 0)
    def _first_step():
        phb_ref[...] = jnp.zeros_like(phb_ref)
        st_ref[...] = jnp.zeros_like(st_ref)

    _interleave(_project_steps(x_ref, g_ref, w_ref, slice(0, tm), pha_ref, pa_ref),
                _hgrn_steps(consts, phb_ref, ng_ref, st_ref, o_ref, 0))
    keep = jnp.where(lax.rem(2 * i, tiles_per_seq) == 0, 0.0, 1.0)
    st_ref[...] = st_ref[...] * keep
    _interleave(_project_steps(x_ref, g_ref, w_ref, slice(tm, 2 * tm), phb_ref, pa_ref),
                _hgrn_steps(consts, pha_ref, ng_ref, st_ref, o_ref, tm))


def _slab_spec(shape, steps):
    span = 1
    while shape[0] % (steps // span) or (shape[0] // (steps // span)) % BF16_SUBLANES:
        span *= 2
        assert span <= steps, shape
    rows = shape[0] // (steps // span)
    return pl.BlockSpec((rows, shape[1]), lambda i: (jnp.minimum(i, steps - 1) // span, 0))


def _proj_hgrn(x2d, gain, w_in, lower_bounds, norm_g, layer, seq, later_weights, tm=ROW_TILE):
    t, d = x2d.shape
    n_h, n_a = HGRN_GROUPS * GROUP, ATTN_GROUPS * GROUP
    n_lb = lower_bounds.shape[0]
    tiles_per_seq = seq // tm
    assert seq % tm == 0 and tiles_per_seq % 2 == 0
    steps = t // (2 * tm)
    last = steps - 1
    slabs = [_slab_spec(w.shape, steps) for w in later_weights]
    out = pl.pallas_call(
        functools.partial(_proj_hgrn_kernel, layer, tiles_per_seq, len(later_weights)),
        grid=(steps + 1,),
        in_specs=[pl.BlockSpec((2 * tm, d), lambda i: (jnp.minimum(i, last), 0)),
                  _resident((1, d)),
                  _resident((d, n_h + n_a)),
                  _resident((n_lb, GROUP)), _resident((1, GROUP))] + slabs,
        out_specs=[pl.BlockSpec((2 * tm, n_a), lambda i: (jnp.minimum(i, last), 0)),
                   pl.BlockSpec((2 * tm, GROUP), lambda i: (i, 0))] + slabs,
        out_shape=[jax.ShapeDtypeStruct((t, n_a), BF16),
                   jax.ShapeDtypeStruct((t + 2 * tm, GROUP), BF16)]
                  + [jax.ShapeDtypeStruct(w.shape, BF16) for w in later_weights],
        scratch_shapes=[pltpu.VMEM((tm, n_h), F32), pltpu.VMEM((tm, n_h), F32),
                        pltpu.VMEM((PAIRS, LANES, LANES), F32)],
        compiler_params=pltpu.CompilerParams(
            dimension_semantics=("arbitrary",), vmem_limit_bytes=VMEM_LIMIT),
        name="proj_hgrn2",
    )(x2d, gain, w_in, lower_bounds, norm_g, *later_weights)
    return out[0], out[1], out[2:]


def _attn_kernel(q_ref, k_ref, v_ref, tri_ref, ng_ref, o_ref, kk_ref, vv_ref, run_ref, acc_ref):
    qi = pl.program_id(1)
    n_blocks = k_ref.shape[1] // KEY_BLOCK
    diag_blocks = Q_BLOCK // KEY_BLOCK
    lane = lax.broadcasted_iota(jnp.int32, (1, LANES), 1)
    first_head = lane < HEAD_DIM
    streams = [(bi, slice(p * LANES, (p + 1) * LANES))
               for bi in range(q_ref.shape[0]) for p in range(PAIRS)]

    @pl.when(qi == 0)
    def _split_heads():
        dim_first_head = lax.broadcasted_iota(jnp.int32, (LANES, KEY_BLOCK), 0) < HEAD_DIM

        def body(j, carry):
            rows = pl.ds(pl.multiple_of(j * KEY_BLOCK, KEY_BLOCK), KEY_BLOCK)
            for s, (bi, cols) in enumerate(streams):
                kt = k_ref[bi, rows, cols].T
                vb = v_ref[bi, rows, cols]
                zero = jnp.zeros_like(vb)
                kk_ref[s, j, :, 0:KEY_BLOCK] = jnp.where(dim_first_head, kt, zero)
                kk_ref[s, j, :, KEY_BLOCK:2 * KEY_BLOCK] = jnp.where(dim_first_head, zero, kt)
                vv_ref[s, j, 0:KEY_BLOCK, :] = jnp.where(first_head, vb, zero)
                vv_ref[s, j, KEY_BLOCK:2 * KEY_BLOCK, :] = jnp.where(first_head, zero, vb)
            return carry
        lax.fori_loop(0, n_blocks, body, 0, unroll=2)

    mask_row = lax.broadcasted_iota(jnp.int32, (KEY_BLOCK, 2 * KEY_BLOCK), 0)
    mask_col = lax.broadcasted_iota(jnp.int32, (KEY_BLOCK, 2 * KEY_BLOCK), 1)
    strictly_before = (mask_col & (KEY_BLOCK - 1)) < mask_row

    def causal(a, diagonal):
        if not diagonal:
            return a
        top = jnp.where(strictly_before, a[:KEY_BLOCK], 0.0)
        return top if a.shape[0] == KEY_BLOCK else jnp.concatenate([top, a[KEY_BLOCK:]], axis=0)

    def scores(j, r0, r1, diagonal):
        zs, cs = [], []
        for s, (bi, cols) in enumerate(streams):
            q = q_ref[bi, r0:r1, cols]
            z = jnp.dot(q, kk_ref[s, j], preferred_element_type=F32)
            zb = z.astype(BF16)
            pos = jnp.maximum(zb, 0.0)
            neg = zb - pos
            c = causal(pos + jnp.log(1.0 + jnp.exp(neg - pos)), diagonal)
            zs.append(z)
            cs.append(c)
        incl_all = jnp.dot(jnp.concatenate(cs, axis=0), tri_ref[...], preferred_element_type=F32)
        return zs, incl_all

    def accumulate(j, r0, r1, diagonal, zs, incl_all):
        rows = r1 - r0

        def earlier(ref, s):
            if not diagonal:
                return ref[s, r0:r1]
            top = jnp.zeros((KEY_BLOCK, ref.shape[2]), F32)
            return top if rows == KEY_BLOCK else jnp.concatenate([top, ref[s, r0 + KEY_BLOCK:r1]], axis=0)

        for s in range(len(streams)):
            incl = incl_all[s * rows:(s + 1) * rows]
            run = earlier(run_ref, s)
            pr = causal(jnp.exp(zs[s] - incl - run), diagonal)
            pv = jnp.dot(pr.astype(BF16), vv_ref[s, j], preferred_element_type=F32)
            tot = jnp.concatenate(
                [jnp.broadcast_to(incl[:, 0:1], (rows, KEY_BLOCK)),
                 jnp.broadcast_to(incl[:, KEY_BLOCK:KEY_BLOCK + 1], (rows, KEY_BLOCK))], axis=1)
            run_ref[s, r0:r1] = run + tot
            acc_ref[s, r0:r1] = earlier(acc_ref, s) + pv

    def block(j, r0, r1, diagonal):
        accumulate(j, r0, r1, diagonal, *scores(j, r0, r1, diagonal))

    def tiles_needed(lowest=0):
        need = jnp.int32(lowest)
        for t in range(lowest, diag_blocks):
            live = jnp.min(functools.reduce(jnp.minimum, [
                run_ref[s, t * KEY_BLOCK:(t + 1) * KEY_BLOCK] for s in range(len(streams))]))
            need = jnp.where(live < RUN_CUTOFF, t + 1, need)
        return need

    def limited_block(j, r0, need, least, diagonal):
        branches = [functools.partial(block, r0=r0, r1=n * KEY_BLOCK, diagonal=diagonal)
                    for n in range(least, diag_blocks + 1)]
        lax.switch(need - least, branches, j)

    first_key_block = qi * diag_blocks
    always = [d for d in reversed(range(diag_blocks)) if d + DIAG_ALWAYS_TILES >= diag_blocks]
    staged = [scores(first_key_block + d, d * KEY_BLOCK, Q_BLOCK, True) for d in always]
    for d, stage in zip(always, staged):
        accumulate(first_key_block + d, d * KEY_BLOCK, Q_BLOCK, True, *stage)
    for d in reversed(range(diag_blocks)):
        least = d + DIAG_ALWAYS_TILES
        if least < diag_blocks:
            limited_block(first_key_block + d, d * KEY_BLOCK, tiles_needed(least), least, True)

    def more_blocks(state):
        t, need = state
        return jnp.logical_and(t < first_key_block, need > 0)

    def off_diagonal(state):
        t, need = state
        limited_block(first_key_block - 1 - t, 0, need, 1, False)
        return t + 1, tiles_needed()

    lax.while_loop(more_blocks, off_diagonal, (jnp.int32(0), tiles_needed()))
    for s, (bi, cols) in enumerate(streams):
        o_ref[bi, :, cols] = (_pair_rms(acc_ref[s], first_head) * ng_ref[:, cols]).astype(o_ref.dtype)


def _attention(proj_a, tri, norm_g):
    b, s, _ = proj_a.shape
    n_blocks = s // KEY_BLOCK
    assert b % ATTN_ROWS == 0
    n_streams = ATTN_ROWS * PAIRS
    return pl.pallas_call(
        _attn_kernel,
        grid=(b // ATTN_ROWS, s // Q_BLOCK),
        in_specs=[pl.BlockSpec((ATTN_ROWS, Q_BLOCK, GROUP), lambda bi, i: (bi, i, 0)),
                  pl.BlockSpec((ATTN_ROWS, s, GROUP), lambda bi, i: (bi, 0, 1)),
                  pl.BlockSpec((ATTN_ROWS, s, GROUP), lambda bi, i: (bi, 0, 2)),
                  _resident((2 * KEY_BLOCK, 2 * KEY_BLOCK)),
                  _resident((1, GROUP))],
        out_specs=pl.BlockSpec((ATTN_ROWS, Q_BLOCK, GROUP), lambda bi, i: (bi, i, 0)),
        out_shape=jax.ShapeDtypeStruct((b, s, GROUP), BF16),
        scratch_shapes=[pltpu.VMEM((n_streams, n_blocks, LANES, 2 * KEY_BLOCK), BF16),
                        pltpu.VMEM((n_streams, n_blocks, 2 * KEY_BLOCK, LANES), BF16),
                        pltpu.VMEM((n_streams, Q_BLOCK, 2 * KEY_BLOCK), F32),
                        pltpu.VMEM((n_streams, Q_BLOCK, LANES), F32)],
        compiler_params=pltpu.CompilerParams(
            dimension_semantics=("arbitrary", "arbitrary"), vmem_limit_bytes=ATTN_VMEM_LIMIT),
        name="stickbreak_attn",
    )(proj_a, proj_a, proj_a, tri, norm_g)


def _suffix_sum_matrix():
    r = jnp.arange(2 * KEY_BLOCK)[:, None]
    c = jnp.arange(2 * KEY_BLOCK)[None, :]
    return ((r >= c) & ((r < KEY_BLOCK) == (c < KEY_BLOCK))).astype(BF16)


def _out_ffn_kernel(final, x_ref, ma0_ref, ma1_ref, mb_ref, wo_ref, g2_ref, wg_ref, wu_ref, wd_ref,
                    g3_ref, o_ref, ff_ref):
    tm = ma0_ref.shape[0]
    tiles = [(slice(0, tm), ma0_ref), (slice(tm, 2 * tm), ma1_ref)]
    for rows, ma_ref in tiles:
        o_ref[rows, :] = (x_ref[rows, :]
                          + jnp.dot(ma_ref[...], wo_ref[0:GROUP, :], preferred_element_type=F32)
                          + jnp.dot(mb_ref[rows, :], wo_ref[GROUP:2 * GROUP, :],
                                    preferred_element_type=F32))
    hs = [_rms(o_ref[rows, :], g2_ref[...]).astype(BF16) for rows, _ in tiles]
    d_ff = wg_ref.shape[1]
    for c in range(d_ff // FF_CHUNK):
        cols = slice(c * FF_CHUNK, (c + 1) * FF_CHUNK)
        for (rows, _), h in zip(tiles, hs):
            gate = jnp.dot(h, wg_ref[:, cols], preferred_element_type=F32)
            up = jnp.dot(h, wu_ref[:, cols], preferred_element_type=F32)
            ff_ref[rows, cols] = (gate * jax.nn.sigmoid(gate) * up).astype(BF16)
    for rows, _ in tiles:
        x2 = o_ref[rows, :] + jnp.dot(ff_ref[rows, :], wd_ref[...], preferred_element_type=F32)
        o_ref[rows, :] = _rms(x2, g3_ref[...]) if final else x2


def _out_ffn(x2d, mix_a, mix_b, w_out, g2, w_gate, w_up, w_down, g3, final, tm=ROW_TILE):
    t, d = x2d.shape
    d_ff = w_gate.shape[1]
    assert t % (2 * tm) == 0
    row = lambda n: pl.BlockSpec((2 * tm, n), lambda i: (i, 0))
    return pl.pallas_call(
        functools.partial(_out_ffn_kernel, final),
        grid=(t // (2 * tm),),
        in_specs=[row(d),
                  pl.BlockSpec((tm, GROUP), lambda i: (2 * i + 1, 0)),
                  pl.BlockSpec((tm, GROUP), lambda i: (2 * i + 2, 0)),
                  row(GROUP),
                  _resident((2 * GROUP, d)), _resident((1, d)),
                  _resident((d, d_ff)), _resident((d, d_ff)), _resident((d_ff, d)),
                  _resident((1, d))],
        out_specs=row(d),
        out_shape=jax.ShapeDtypeStruct((t, d), F32),
        scratch_shapes=[pltpu.VMEM((2 * tm, d_ff), BF16)],
        compiler_params=pltpu.CompilerParams(
            dimension_semantics=("arbitrary",), vmem_limit_bytes=ATTN_VMEM_LIMIT),
        name="out_ffn",
    )(x2d, mix_a, mix_a, mix_b, w_out, g2, w_gate, w_up, w_down, g3)


def kernel(x, mix_norm_g, w_in, lower_bounds, hgrn_norm_g, sb_norm_g, w_out, ffn_norm_g,
           w_gate, w_up, w_down, final_norm_g):
    b, s, d = x.shape
    depth = w_in.shape[0]
    assert w_in.shape[2] == (HGRN_GROUPS + ATTN_GROUPS) * GROUP
    assert w_gate.shape[2] % FF_CHUNK == 0 and s % KEY_BLOCK == 0
    tri = _suffix_sum_matrix()
    x2d = x.reshape(b * s, d)
    for l in range(depth):
        proj_a, mix_a, (wo, wg, wu, wd) = _proj_hgrn(
            x2d, mix_norm_g[l][None], w_in[l], lower_bounds, hgrn_norm_g[l][None], l, s,
            (w_out[l], w_gate[l], w_up[l], w_down[l]))
        mix_b = _attention(proj_a.reshape(b, s, -1), tri, sb_norm_g[l][None])
        last = l == depth - 1
        g3 = final_norm_g[None] if last else ffn_norm_g[l][None]
        x2d = _out_ffn(x2d, mix_a, mix_b.reshape(b * s, -1), wo, ffn_norm_g[l][None], wg, wu, wd,
                       g3, last)
    if depth == 0:
        raise ValueError("depth must be positive")
    return x2d.reshape(b, s, d)
```
